```python
import math
import jax, jax.numpy as jnp
from jax import lax
import numpy as np

D_MODEL = 1024
BATCH = 8
SEQ = 2048
DEPTH = 4
DEC_BATCH = 128
DEC_SEQ = 4
PAST_LEN = 2048
PAGE_SIZE = 128

A_GROUPS = ((128, 1), (512, 4), (2048, 16))
N_GROUPS_A = len(A_GROUPS)
HG_A = 6
HD_A = 64
ROPE_THETA = 10000.0
A_PROJ = N_GROUPS_A * 3 * HG_A * HD_A
A_OUT = N_GROUPS_A * HG_A * HD_A
H_B = D_MODEL // 128
DK_B = 128
DV_B = 128
CONV_W = 4
CHUNK_B = 64
B_QKV = 3 * H_B * DK_B
B_PROJ = B_QKV + H_B * DV_B + 2 * H_B
POOL_WINDOWS = (2, 4, 8, 16)
POOL_CG = D_MODEL // len(POOL_WINDOWS)
POOL_BUF = max(POOL_WINDOWS) - 1
D_FF = 4 * D_MODEL
N_A = (DEPTH + 2) // 3
N_B = (DEPTH + 1) // 3
N_C = DEPTH // 3
RMS_EPS = 1e-6

kernel_name = 'hybrid_dilated_delta_pool_step'


def rmsnorm(x, g):
    xf = x.astype(jnp.float32)
    y = xf * lax.rsqrt(jnp.mean(xf * xf, axis=-1, keepdims=True) + RMS_EPS)
    return (y * g.astype(jnp.float32)).astype(x.dtype)


def l2norm(x):
    return x * lax.rsqrt(jnp.sum(x * x, axis=-1, keepdims=True) + 1e-6)


def rope(x, pos):
    half = x.shape[-1] // 2
    inv = ROPE_THETA ** (-jnp.arange(half, dtype=jnp.float32) / half)
    ang = pos.astype(jnp.float32)[:, None] * inv[None, :]
    cos = jnp.cos(ang)[:, None, :]
    sin = jnp.sin(ang)[:, None, :]
    xf = x.astype(jnp.float32)
    x1, x2 = xf[..., :half], xf[..., half:]
    return jnp.concatenate([x1 * cos - x2 * sin, x2 * cos + x1 * sin], axis=-1)


def dilated_window_prompt(q, k, v, window, dil):
    B, S, H, hd = q.shape
    n = S // dil
    blk = window // dil
    nb = -(-n // blk)
    npad = nb * blk

    def to_blocks(x):
        x = x.astype(jnp.float32).reshape(B, n, dil, H, hd).transpose(0, 2, 1, 3, 4)
        x = jnp.pad(x, ((0, 0), (0, 0), (0, npad - n), (0, 0), (0, 0)))
        return x.reshape(B, dil, nb, blk, H, hd)

    def with_prev(x):
        prev = jnp.concatenate([jnp.zeros_like(x[:, :, :1]), x[:, :, :-1]], axis=2)
        return jnp.concatenate([prev, x], axis=3)

    qb = to_blocks(q)
    kk = with_prev(to_blocks(k))
    vv = with_prev(to_blocks(v))
    s = jnp.einsum('brnqhd,brnkhd->brnhqk', qb, kk) * (hd ** -0.5)
    qi = jnp.arange(blk)[:, None]
    kj = jnp.arange(2 * blk)[None, :]
    dist = blk + qi - kj
    band = (dist >= 0) & (dist <= blk)
    has_prev = (jnp.arange(nb)[:, None, None] > 0) | (kj[None] >= blk)
    mask = band[None] & has_prev
    s = jnp.where(mask[None, None, :, None], s, -jnp.inf)
    m = jnp.max(s, axis=-1)
    p = jnp.exp(s - m[..., None])
    l = jnp.sum(p, axis=-1)
    o = jnp.einsum('brnhqk,brnkhd->brnqhd', p, vv) / jnp.swapaxes(l, -1, -2)[..., None]
    lse = jnp.swapaxes(m + jnp.log(l), -1, -2)

    def from_blocks(x):
        x = x.reshape((B, dil, npad) + x.shape[4:])[:, :, :n]
        x = jnp.swapaxes(x, 1, 2)
        return x.reshape((B, S) + x.shape[3:])

    return from_blocks(o), from_blocks(lse)


def dilated_window_sample(q, k, v, buf, window, dil):
    T = q.shape[1]
    L = buf.shape[1]
    hd = q.shape[-1]
    kc = jnp.concatenate([buf[:, :, 0].astype(jnp.float32), k], axis=1)
    vc = jnp.concatenate([buf[:, :, 1].astype(jnp.float32), v], axis=1)
    offs = jnp.arange(window // dil + 1) * dil
    idx = L + jnp.arange(T)[:, None] - offs[None, :]
    valid = idx >= 0
    idx = jnp.maximum(idx, 0)
    kg = kc[:, idx]
    vg = vc[:, idx]
    s = jnp.einsum('bthd,btmhd->bthm', q.astype(jnp.float32), kg) * (hd ** -0.5)
    s = jnp.where(valid[None, :, None, :], s, -jnp.inf)
    m = jnp.max(s, axis=-1)
    p = jnp.exp(s - m[..., None])
    l = jnp.sum(p, axis=-1)
    o = jnp.einsum('bthm,btmhd->bthd', p, vg) / l[..., None]
    new_buf = jnp.stack([kc[:, T:], vc[:, T:]], axis=2)
    return o, m + jnp.log(l), new_buf


def mixer_a(h, w_in, w_out, pos, bufs):
    B, T, _ = h.shape
    qkv = (h @ w_in).reshape(B, T, N_GROUPS_A, 3, HG_A, HD_A)
    outs, lses, new_bufs = [], [], []
    for g, (window, dil) in enumerate(A_GROUPS):
        q = rope(qkv[:, :, g, 0], pos)
        k = rope(qkv[:, :, g, 1], pos)
        v = qkv[:, :, g, 2].astype(jnp.float32)
        if bufs is None:
            o, lse = dilated_window_prompt(q, k, v, window, dil)
            keep = min(window, T)
            nbuf = jnp.stack([k[:, T - keep:], v[:, T - keep:]], axis=2)
        else:
            o, lse, nbuf = dilated_window_sample(q, k, v, bufs[g], window, dil)
        outs.append(o)
        lses.append(lse)
        new_bufs.append(nbuf)
    wts = jax.nn.softmax(jnp.stack(lses, axis=2), axis=2)
    o = (jnp.stack(outs, axis=2) * wts[..., None]).reshape(B, T, A_OUT)
    return o.astype(h.dtype) @ w_out, new_bufs


def gated_delta_chunked(q, k, v, g, beta, s0):
    B, T, H, dk = q.shape
    dv = v.shape[-1]
    C = min(CHUNK_B, T)
    N = -(-T // C)
    pad = N * C - T

    def prep(x):
        x = jnp.pad(x.astype(jnp.float32), [(0, 0), (0, pad)] + [(0, 0)] * (x.ndim - 2))
        x = x.reshape((B, N, C, H) + x.shape[3:])
        return jnp.moveaxis(x, 3, 1)

    q, k, v, g, beta = prep(q), prep(k), prep(v), prep(g), prep(beta)
    G = jnp.cumsum(g, axis=-1)
    tri = jnp.tril(jnp.ones((C, C), bool))
    stri = jnp.tril(jnp.ones((C, C), bool), -1)
    dec = jnp.exp(jnp.where(tri, G[..., :, None] - G[..., None, :], -jnp.inf))
    kb = k * beta[..., None]
    A = jnp.where(stri, jnp.einsum('bhncd,bhnsd->bhncs', kb, k) * dec, 0.0)
    eye = jnp.eye(C, dtype=jnp.float32)
    Tm = lax.linalg.triangular_solve(eye + A, jnp.broadcast_to(eye, A.shape), left_side=True, lower=True)
    u = jnp.einsum('bhncs,bhnse->bhnce', Tm, v * beta[..., None])
    w = jnp.einsum('bhncs,bhnsd->bhncd', Tm, kb * jnp.exp(G)[..., None])
    attn = jnp.einsum('bhncd,bhnsd->bhncs', q, k) * dec
    qe = q * jnp.exp(G)[..., None]
    Gl = G[..., -1:]
    kd = k * jnp.exp(Gl - G)[..., None]
    dl = jnp.exp(Gl[..., 0])

    def step(S, xs):
        qe_i, u_i, w_i, attn_i, kd_i, dl_i = xs
        v_new = u_i - jnp.einsum('bhcd,bhde->bhce', w_i, S)
        o_i = jnp.einsum('bhcd,bhde->bhce', qe_i, S) + jnp.einsum('bhcs,bhse->bhce', attn_i, v_new)
        S = S * dl_i[..., None, None] + jnp.einsum('bhcd,bhce->bhde', kd_i, v_new)
        return S, o_i

    xs = tuple(jnp.moveaxis(x, 2, 0) for x in (qe, u, w, attn, kd, dl))
    S, o = lax.scan(step, s0.astype(jnp.float32), xs)
    o = jnp.moveaxis(jnp.moveaxis(o, 0, 2), 1, 3).reshape(B, N * C, H, dv)[:, :T]
    return o, S


def mixer_b(h, conv_buf, s0, w_in, conv_w, a_log, dt_bias, norm_w, w_out):
    B, T, _ = h.shape
    proj = h @ w_in
    o1 = B_QKV + H_B * DV_B
    xqkv = proj[..., :B_QKV]
    z = proj[..., B_QKV:o1].reshape(B, T, H_B, DV_B).astype(jnp.float32)
    a = proj[..., o1:o1 + H_B].astype(jnp.float32)
    b = proj[..., o1 + H_B:].astype(jnp.float32)
    xp = jnp.concatenate([conv_buf.astype(xqkv.dtype), xqkv], axis=1)
    c = xp[:, 0:T] * conv_w[0]
    for j in range(1, CONV_W):
        c = c + xp[:, j:j + T] * conv_w[j]
    c = jax.nn.silu(c.astype(jnp.float32))
    q = l2norm(c[..., :H_B * DK_B].reshape(B, T, H_B, DK_B)) * (DK_B ** -0.5)
    k = l2norm(c[..., H_B * DK_B:2 * H_B * DK_B].reshape(B, T, H_B, DK_B))
    v = c[..., 2 * H_B * DK_B:].reshape(B, T, H_B, DV_B)
    beta = jax.nn.sigmoid(b)
    g = -jnp.exp(a_log.astype(jnp.float32)) * jax.nn.softplus(a + dt_bias.astype(jnp.float32))
    o, s_new = gated_delta_chunked(q, k, v, g, beta, s0)
    o = rmsnorm(o, norm_w) * jax.nn.silu(z)
    y = o.reshape(B, T, H_B * DV_B).astype(h.dtype) @ w_out
    return y, xp[:, xp.shape[1] - (CONV_W - 1):], s_new


def mixer_c(h, prefix, n_valid, w_grp, scale):
    B, T, D = h.shape
    P = prefix.shape[1]
    xp = jnp.concatenate([prefix.astype(h.dtype), h], axis=1)
    xpf = xp.astype(jnp.float32)
    cs = jnp.concatenate([jnp.zeros((B, 1, D), jnp.float32), jnp.cumsum(xpf, axis=1)], axis=1)
    t = jnp.arange(T)
    end = P + t + 1
    hf = xpf[:, P:]
    diffs = []
    for gi, w in enumerate(POOL_WINDOWS):
        lo, hi = gi * POOL_CG, (gi + 1) * POOL_CG
        tot = cs[:, end, lo:hi] - cs[:, end - w, lo:hi]
        cnt = jnp.minimum(w, t + 1 + n_valid).astype(jnp.float32)
        diffs.append(tot / cnt[None, :, None] - hf[..., lo:hi])
    d = jnp.stack(diffs, axis=2)
    y = jnp.einsum('btgc,gce->btge', d, w_grp.astype(jnp.float32)).reshape(B, T, D) * scale.astype(jnp.float32)
    return y.astype(h.dtype), xp[:, T:]


def sq_relu_mlp(h, w1, w2):
    return jnp.square(jax.nn.relu(h @ w1)) @ w2


def setup_inputs(seed: int = 0) -> dict:
    key = jax.random.key(seed)
    ks = iter(jax.random.split(key, 32))

    def nrm(shape, scale=1.0):
        return jax.random.normal(next(ks), shape, jnp.float32) * scale

    x_prompt = nrm((BATCH, SEQ, D_MODEL))
    x_sample = nrm((DEC_BATCH, DEC_SEQ, D_MODEL))
    cache_a_kv1 = nrm((N_A, DEC_BATCH, min(A_GROUPS[0][0], PAST_LEN), 2, HG_A, HD_A))
    cache_a_kv2 = nrm((N_A, DEC_BATCH, min(A_GROUPS[1][0], PAST_LEN), 2, HG_A, HD_A))
    cache_a_kv3 = nrm((N_A, DEC_BATCH, min(A_GROUPS[2][0], PAST_LEN), 2, HG_A, HD_A))
    state_b_s = nrm((N_B, DEC_BATCH, H_B, DK_B, DV_B), 0.1)
    state_b_conv = nrm((N_B, DEC_BATCH, CONV_W - 1, B_QKV))
    state_c_pool = nrm((N_C, DEC_BATCH, min(POOL_BUF, PAST_LEN), D_MODEL))
    norm1 = 1.0 + nrm((DEPTH, D_MODEL), 0.02)
    norm2 = 1.0 + nrm((DEPTH, D_MODEL), 0.02)
    norm_f = 1.0 + nrm((D_MODEL,), 0.02)
    a_w_in = nrm((N_A, D_MODEL, A_PROJ), D_MODEL ** -0.5)
    a_w_out = nrm((N_A, A_OUT, D_MODEL), A_OUT ** -0.5)
    b_w_in = nrm((N_B, D_MODEL, B_PROJ), D_MODEL ** -0.5)
    b_conv = nrm((N_B, CONV_W, B_QKV), CONV_W ** -0.5)
    b_a_log = jnp.log(jax.random.uniform(next(ks), (N_B, H_B), jnp.float32, 1.0, 16.0))
    dt = jnp.exp(jax.random.uniform(next(ks), (N_B, H_B), jnp.float32) * (math.log(0.1) - math.log(0.001)) + math.log(0.001))
    b_dt_bias = dt + jnp.log(-jnp.expm1(-dt))
    b_norm = 1.0 + nrm((N_B, DV_B), 0.02)
    b_w_out = nrm((N_B, H_B * DV_B, D_MODEL), (H_B * DV_B) ** -0.5)
    c_w = nrm((N_C, len(POOL_WINDOWS), POOL_CG, POOL_CG), POOL_CG ** -0.5)
    c_scale = 1.0 + nrm((N_C, D_MODEL), 0.1)
    mlp_w1 = nrm((DEPTH, D_MODEL, D_FF), D_MODEL ** -0.5)
    mlp_w2 = nrm((DEPTH, D_FF, D_MODEL), D_FF ** -0.5)
    return {'x_prompt': x_prompt, 'x_sample': x_sample,
            'cache_a_kv1': cache_a_kv1, 'cache_a_kv2': cache_a_kv2, 'cache_a_kv3': cache_a_kv3,
            'state_b_s': state_b_s, 'state_b_conv': state_b_conv, 'state_c_pool': state_c_pool,
            'norm1': norm1, 'norm2': norm2, 'norm_f': norm_f,
            'a_w_in': a_w_in, 'a_w_out': a_w_out,
            'b_w_in': b_w_in, 'b_conv': b_conv, 'b_a_log': b_a_log, 'b_dt_bias': b_dt_bias,
            'b_norm': b_norm, 'b_w_out': b_w_out,
            'c_w': c_w, 'c_scale': c_scale,
            'mlp_w1': mlp_w1, 'mlp_w2': mlp_w2}


def reference(x_prompt, x_sample, cache_a_kv1, cache_a_kv2, cache_a_kv3, state_b_s, state_b_conv, state_c_pool,
              norm1, norm2, norm_f, a_w_in, a_w_out, b_w_in, b_conv, b_a_log, b_dt_bias, b_norm, b_w_out,
              c_w, c_scale, mlp_w1, mlp_w2):
    xp, xs = x_prompt, x_sample
    Bp, Sp = xp.shape[0], xp.shape[1]
    Ts = xs.shape[1]
    pos_p = jnp.arange(Sp)
    pos_s = PAST_LEN + jnp.arange(Ts)
    a_p, a_s, b_p, b_s, c_p, c_s = [], [], [], [], [], []
    for i in range(DEPTH):
        kind, j = i % 3, i // 3
        hp = rmsnorm(xp, norm1[i])
        hs = rmsnorm(xs, norm1[i])
        if kind == 0:
            yp, kv_p = mixer_a(hp, a_w_in[j], a_w_out[j], pos_p, None)
            ys, kv_s = mixer_a(hs, a_w_in[j], a_w_out[j], pos_s, (cache_a_kv1[j], cache_a_kv2[j], cache_a_kv3[j]))
            a_p.append(kv_p)
            a_s.append(kv_s)
        elif kind == 1:
            conv0 = jnp.zeros((Bp, CONV_W - 1, B_QKV), hp.dtype)
            s0 = jnp.zeros((Bp, H_B, DK_B, DV_B), jnp.float32)
            yp, cb_p, sb_p = mixer_b(hp, conv0, s0, b_w_in[j], b_conv[j], b_a_log[j], b_dt_bias[j], b_norm[j], b_w_out[j])
            ys, cb_s, sb_s = mixer_b(hs, state_b_conv[j], state_b_s[j], b_w_in[j], b_conv[j], b_a_log[j], b_dt_bias[j], b_norm[j], b_w_out[j])
            b_p.append((sb_p, cb_p))
            b_s.append((sb_s, cb_s))
        else:
            yp, pool_p = mixer_c(hp, jnp.zeros((Bp, POOL_BUF, D_MODEL), hp.dtype), 0, c_w[j], c_scale[j])
            ys, pool_s = mixer_c(hs, state_c_pool[j], state_c_pool.shape[2], c_w[j], c_scale[j])
            c_p.append(pool_p)
            c_s.append(pool_s)
        xp = xp + yp
        xs = xs + ys
        xp = xp + sq_relu_mlp(rmsnorm(xp, norm2[i]), mlp_w1[i], mlp_w2[i])
        xs = xs + sq_relu_mlp(rmsnorm(xs, norm2[i]), mlp_w1[i], mlp_w2[i])
    y_prompt = rmsnorm(xp, norm_f)
    y_sample = rmsnorm(xs, norm_f)
    new_a_kv1_p = jnp.stack([e[0] for e in a_p])
    new_a_kv2_p = jnp.stack([e[1] for e in a_p])
    new_a_kv3_p = jnp.stack([e[2] for e in a_p])
    new_b_s_p = jnp.stack([e[0] for e in b_p])
    new_b_conv_p = jnp.stack([e[1] for e in b_p])
    new_c_pool_p = jnp.stack(c_p)
    new_a_kv1_s = jnp.stack([e[0] for e in a_s])
    new_a_kv2_s = jnp.stack([e[1] for e in a_s])
    new_a_kv3_s = jnp.stack([e[2] for e in a_s])
    new_b_s_s = jnp.stack([e[0] for e in b_s])
    new_b_conv_s = jnp.stack([e[1] for e in b_s])
    new_c_pool_s = jnp.stack(c_s)
    return (y_prompt, y_sample, new_a_kv1_p, new_a_kv2_p, new_a_kv3_p, new_b_s_p, new_b_conv_p, new_c_pool_p, new_a_kv1_s, new_a_kv2_s, new_a_kv3_s, new_b_s_s, new_b_conv_s, new_c_pool_s)
```

```python
import functools
import math

import numpy as np
import jax
import jax.numpy as jnp
from jax import lax
from jax.experimental import pallas as pl
from jax.experimental.pallas import tpu as pltpu

F32 = jnp.float32
BF16 = jnp.bfloat16

RMS_EPS = 1e-6
L2_EPS = 1e-6
ROPE_THETA = 10000.0
A_GROUPS = ((128, 1), (512, 4), (2048, 16))
HG_A = 6
HD_A = 64
A_GW = HG_A * HD_A
A_BLK = 128
CONV_W = 4
DK_B = 128
CHUNK_B = 64
POOL_WINDOWS = (2, 4, 8, 16)
POOL_BUF = max(POOL_WINDOWS) - 1
LANES = 128
VMEM_LIMIT = 56 * 1024 * 1024
NEG_INF = float("-inf")


def _cparams(*sem):
    return pltpu.CompilerParams(dimension_semantics=sem, vmem_limit_bytes=VMEM_LIMIT)


def _bdot(a, b):
    return jnp.dot(a.astype(BF16), b.astype(BF16), preferred_element_type=F32)


def _bdot_nt(a, b):
    return lax.dot_general(a.astype(BF16), b.astype(BF16), (((1,), (1,)), ((), ())),
                           preferred_element_type=F32)


def _bdot_tn(a, b):
    return lax.dot_general(a.astype(BF16), b.astype(BF16), (((0,), (0,)), ((), ())),
                           preferred_element_type=F32)


def _rms(x, g):
    return x * lax.rsqrt(jnp.mean(x * x, axis=-1, keepdims=True) + RMS_EPS) * g


def _silu(x):
    return x * (1.0 / (1.0 + jnp.exp(-x)))


def _iota(shape, dim):
    return lax.broadcasted_iota(jnp.int32, shape, dim)


def _row_tile(m, want):
    t = min(m, want)
    assert m % t == 0, (m, t)
    return t


def _norm_mm_kernel(x_ref, g_ref, w_ref, o_ref):
    h = _rms(x_ref[...], g_ref[...])
    o_ref[...] = _bdot(h, w_ref[...])


def norm_mm(x, g, w, tm=512):
    m, d = x.shape
    n = w.shape[1]
    tm = _row_tile(m, tm)
    return pl.pallas_call(
        _norm_mm_kernel,
        grid=(m // tm,),
        in_specs=[pl.BlockSpec((tm, d), lambda i: (i, 0)),
                  pl.BlockSpec((1, d), lambda i: (0, 0)),
                  pl.BlockSpec((d, n), lambda i: (0, 0))],
        out_specs=pl.BlockSpec((tm, n), lambda i: (i, 0)),
        out_shape=jax.ShapeDtypeStruct((m, n), F32),
        compiler_params=_cparams("parallel"),
        name="norm_mm",
    )(x, g.reshape(1, d), w)


def _mm_res_kernel(a_ref, w_ref, r_ref, o_ref):
    o_ref[...] = r_ref[...] + _bdot(a_ref[...], w_ref[...])


def mm_res(a, w, r, tm=512):
    m, k = a.shape
    n = w.shape[1]
    tm = _row_tile(m, tm)
    return pl.pallas_call(
        _mm_res_kernel,
        grid=(m // tm,),
        in_specs=[pl.BlockSpec((tm, k), lambda i: (i, 0)),
                  pl.BlockSpec((k, n), lambda i: (0, 0)),
                  pl.BlockSpec((tm, n), lambda i: (i, 0))],
        out_specs=pl.BlockSpec((tm, n), lambda i: (i, 0)),
        out_shape=jax.ShapeDtypeStruct((m, n), F32),
        compiler_params=_cparams("parallel"),
        name="mm_res",
    )(a, w, r)


def _mlp_kernel(x_ref, g_ref, w1_ref, w2_ref, gf_ref, o_ref, *, fc, final_norm):
    x = x_ref[...]
    h = _rms(x, g_ref[...]).astype(BF16)
    acc = x
    for c in range(w1_ref.shape[1] // fc):
        a = jnp.dot(h, w1_ref[:, c * fc:(c + 1) * fc], preferred_element_type=F32)
        a = jnp.square(jnp.maximum(a, 0.0))
        acc = acc + _bdot(a, w2_ref[c * fc:(c + 1) * fc, :])
    if final_norm:
        acc = _rms(acc, gf_ref[...])
    o_ref[...] = acc


def mlp_res(x, g, w1, w2, gf, final_norm, tm=512, fc=1024):
    m, d = x.shape
    f = w1.shape[1]
    tm = _row_tile(m, tm)
    return pl.pallas_call(
        functools.partial(_mlp_kernel, fc=fc, final_norm=final_norm),
        grid=(m // tm,),
        in_specs=[pl.BlockSpec((tm, d), lambda i: (i, 0)),
                  pl.BlockSpec((1, d), lambda i: (0, 0)),
                  pl.BlockSpec((d, f), lambda i: (0, 0)),
                  pl.BlockSpec((f, d), lambda i: (0, 0)),
                  pl.BlockSpec((1, d), lambda i: (0, 0))],
        out_specs=pl.BlockSpec((tm, d), lambda i: (i, 0)),
        out_shape=jax.ShapeDtypeStruct((m, d), F32),
        compiler_params=_cparams("parallel"),
        name="mlp_res",
    )(x, g.reshape(1, d), w1, w2, gf.reshape(1, d))


def _rope_tables(pos):
    half = HD_A // 2
    inv = ROPE_THETA ** (-jnp.arange(half, dtype=F32) / half)
    ang = pos.astype(F32)[:, None] * inv[None, :]
    cos = jnp.tile(jnp.cos(ang), (1, LANES // half))
    sin = jnp.sin(ang)
    sin = jnp.tile(jnp.concatenate([-sin, sin], axis=1), (1, LANES // HD_A))
    return cos, sin


def _rope_slab(x, cos, sin):
    half = HD_A // 2
    first = (_iota(x.shape, 1) % HD_A) < half
    swapped = jnp.where(first, pltpu.roll(x, LANES - half, 1), pltpu.roll(x, half, 1))
    return x * cos + swapped * sin


def _attn_prompt_kernel(q_ref, k_ref, v_ref, cos_ref, sin_ref, o_ref, lse_ref, kv_ref,
                        qs, ks, vs, *, n):
    nb = n // A_BLK
    cos = cos_ref[...]
    sin = sin_ref[...]
    lane = _iota((A_BLK, LANES), 1)
    head0 = lane < HD_A
    qi = _iota((A_BLK, 2 * A_BLK), 0)
    kj = _iota((A_BLK, 2 * A_BLK), 1)
    dist = A_BLK + qi - kj
    band = (dist >= 0) & (dist <= A_BLK)
    cur = kj >= A_BLK
    scale = HD_A ** -0.5
    ks[0:A_BLK, :] = jnp.zeros((A_BLK, LANES), BF16)
    vs[0:A_BLK, :] = jnp.zeros((A_BLK, LANES), BF16)
    for p in range(A_GW // LANES):
        cols = slice(p * LANES, (p + 1) * LANES)
        k2 = _rope_slab(k_ref[0, :, cols], cos, sin)
        v2 = v_ref[0, :, cols]
        kv_ref[0, :, cols] = k2[n - A_BLK:, :]
        kv_ref[0, :, A_GW + p * LANES:A_GW + (p + 1) * LANES] = v2[n - A_BLK:, :]
        qs[...] = (_rope_slab(q_ref[0, :, cols], cos, sin) * scale).astype(BF16)
        ks[A_BLK:, :] = k2.astype(BF16)
        vs[A_BLK:, :] = v2.astype(BF16)

        def block(i, carry):
            r0 = pl.multiple_of(i * A_BLK, A_BLK)
            q2 = qs[pl.ds(r0, A_BLK), :]
            kc = ks[pl.ds(r0, 2 * A_BLK), :]
            vc = vs[pl.ds(r0, 2 * A_BLK), :]
            mask = band & (cur | (i > 0))
            outs = []
            for hsel in (head0, ~head0):
                s = _bdot_nt(jnp.where(hsel, q2, jnp.zeros_like(q2)), kc)
                s = jnp.where(mask, s, NEG_INF)
                m = jnp.max(s, axis=-1, keepdims=True)
                e = jnp.exp(s - m)
                l = jnp.sum(e, axis=-1, keepdims=True)
                outs.append((_bdot(e, vc) / l, m + jnp.log(l)))
            o_ref[0, pl.ds(r0, A_BLK), cols] = jnp.where(head0, outs[0][0], outs[1][0])
            lse_ref[0, pl.ds(r0, A_BLK), cols] = jnp.where(
                head0, jnp.broadcast_to(outs[0][1], (A_BLK, LANES)),
                jnp.broadcast_to(outs[1][1], (A_BLK, LANES)))
            return carry

        lax.fori_loop(0, nb, block, 0)


def attn_prompt(qkv, g, dil, cos, sin):
    b, s, width = qkv.shape
    n = s // dil
    assert n * dil == s and n % A_BLK == 0
    qkv_v = qkv.reshape(b, n, dil * width)
    cos_v = cos.reshape(n, dil * LANES)
    sin_v = sin.reshape(n, dil * LANES)
    slabs = width // A_GW

    def col_spec(c):
        return pl.BlockSpec((1, n, A_GW), lambda i, r: (i, 0, r * slabs + g * 3 + c))

    tab_spec = pl.BlockSpec((n, LANES), lambda i, r: (0, r))
    out_spec = pl.BlockSpec((1, n, A_GW), lambda i, r: (i, 0, r))
    o, lse, kv = pl.pallas_call(
        functools.partial(_attn_prompt_kernel, n=n),
        grid=(b, dil),
        in_specs=[col_spec(0), col_spec(1), col_spec(2), tab_spec, tab_spec],
        out_specs=[out_spec, out_spec,
                   pl.BlockSpec((1, A_BLK, 2 * A_GW), lambda i, r: (i, 0, r))],
        out_shape=[jax.ShapeDtypeStruct((b, n, dil * A_GW), F32),
                   jax.ShapeDtypeStruct((b, n, dil * A_GW), F32),
                   jax.ShapeDtypeStruct((b, A_BLK, dil * 2 * A_GW), F32)],
        scratch_shapes=[pltpu.VMEM((n, LANES), BF16),
                        pltpu.VMEM((n + A_BLK, LANES), BF16),
                        pltpu.VMEM((n + A_BLK, LANES), BF16)],
        compiler_params=_cparams("parallel", "parallel"),
        name=f"attn_prompt_g{g}",
    )(qkv_v, qkv_v, qkv_v, cos_v, sin_v)
    return o.reshape(b, s, A_GW), lse.reshape(b, s, A_GW), kv


def _attn_sample_kernel(qkv_ref, kv1_ref, kv2_ref, kv3_ref, cos_ref, sin_ref, bd_ref,
                        o_ref, lse_ref, new_ref, *, bb, t_new):
    cos = cos_ref[...]
    sin = sin_ref[...]
    bd = bd_ref[...]
    scale = HD_A ** -0.5
    rows = A_BLK + 8
    rid = _iota((rows, 1), 0)
    kv_refs = (kv1_ref, kv2_ref, kv3_ref)
    pad = jnp.zeros((8 - t_new, A_GW), F32)

    def per_seq(bl, carry):
        x = qkv_ref[bl]
        for g, (_, dil) in enumerate(A_GROUPS):
            base = g * 3 * A_GW

            def roped(c0):
                return jnp.concatenate(
                    [_rope_slab(x[:, c0 + p * LANES:c0 + (p + 1) * LANES], cos, sin)
                     for p in range(A_GW // LANES)], axis=1)

            q = roped(base) * scale
            k_new = roped(base + A_GW)
            v_new = x[:, base + 2 * A_GW:base + 3 * A_GW]
            new_ref[bl, :, g * 2 * A_GW:g * 2 * A_GW + A_GW] = k_new
            new_ref[bl, :, g * 2 * A_GW + A_GW:(g + 1) * 2 * A_GW] = v_new
            k_new8 = jnp.concatenate([k_new, pad], axis=0)
            v_new8 = jnp.concatenate([v_new, pad], axis=0)
            for t in range(t_new):
                c0 = 0 if dil == 1 else t * 2 * A_GW
                kb = kv_refs[g][bl, :, c0:c0 + A_GW]
                vb = kv_refs[g][bl, :, c0 + A_GW:c0 + 2 * A_GW]
                kx = jnp.concatenate([kb, k_new8], axis=0)
                vx = jnp.concatenate([vb, v_new8], axis=0)
                if dil == 1:
                    valid = ((rid >= t) & (rid < A_BLK)) | ((rid >= A_BLK) & (rid <= A_BLK + t))
                else:
                    valid = (rid < A_BLK) | (rid == A_BLK + t)
                prod = kx * q[t:t + 1, :]
                s = jnp.concatenate(
                    [_bdot(prod[:, p * LANES:(p + 1) * LANES], bd)
                     for p in range(A_GW // LANES)], axis=1)
                s = jnp.where(valid, s, NEG_INF)
                m = jnp.max(s, axis=0, keepdims=True)
                e = jnp.exp(s - m)
                l = jnp.sum(e, axis=0, keepdims=True)
                o = jnp.sum(e * vx, axis=0, keepdims=True) / l
                o_ref[bl, t:t + 1, g * A_GW:(g + 1) * A_GW] = o
                lse_ref[bl, t:t + 1, g * A_GW:(g + 1) * A_GW] = m + jnp.log(l)
        return carry

    lax.fori_loop(0, bb, per_seq, 0)


def attn_sample(qkv, kv1, kv2, kv3, cos, sin, bb=4):
    bd_, t_new, width = qkv.shape
    assert t_new <= min(d for _, d in A_GROUPS[1:]) and bd_ % bb == 0
    views = []
    for kv, (window, dil) in zip((kv1, kv2, kv3), A_GROUPS):
        assert kv.shape[1] == window, "buffer shorter than the window is not supported"
        views.append(kv.reshape(bd_, window // dil, dil * 2 * A_GW))
    blockdiag = (np.arange(LANES)[:, None] // HD_A == np.arange(LANES)[None, :] // HD_A)
    blockdiag = jnp.asarray(blockdiag, BF16)
    kv_cols = [2 * A_GW if dil == 1 else t_new * 2 * A_GW for _, dil in A_GROUPS]
    ng = len(A_GROUPS)
    return pl.pallas_call(
        functools.partial(_attn_sample_kernel, bb=bb, t_new=t_new),
        grid=(bd_ // bb,),
        in_specs=[pl.BlockSpec((bb, t_new, width), lambda i: (i, 0, 0))]
                 + [pl.BlockSpec((bb, A_BLK, c), lambda i: (i, 0, 0)) for c in kv_cols]
                 + [pl.BlockSpec((t_new, LANES), lambda i: (0, 0)),
                    pl.BlockSpec((t_new, LANES), lambda i: (0, 0)),
                    pl.BlockSpec((LANES, LANES), lambda i: (0, 0))],
        out_specs=[pl.BlockSpec((bb, t_new, ng * A_GW), lambda i: (i, 0, 0)),
                   pl.BlockSpec((bb, t_new, ng * A_GW), lambda i: (i, 0, 0)),
                   pl.BlockSpec((bb, t_new, ng * 2 * A_GW), lambda i: (i, 0, 0))],
        out_shape=[jax.ShapeDtypeStruct((bd_, t_new, ng * A_GW), F32),
                   jax.ShapeDtypeStruct((bd_, t_new, ng * A_GW), F32),
                   jax.ShapeDtypeStruct((bd_, t_new, ng * 2 * A_GW), F32)],
        compiler_params=_cparams("parallel"),
        name="attn_sample",
    )(qkv, *views, cos, sin, blockdiag)


def _combine_out_kernel(*refs, ng):
    o_refs, l_refs = refs[:ng], refs[ng:2 * ng]
    w_ref, r_ref, out_ref = refs[2 * ng:]
    ls = [l[...] for l in l_refs]
    m = functools.reduce(jnp.maximum, ls)
    es = [jnp.exp(l - m) for l in ls]
    inv = 1.0 / functools.reduce(jnp.add, es)
    a = jnp.concatenate([o[...] * (e * inv) for o, e in zip(o_refs, es)], axis=1)
    out_ref[...] = r_ref[...] + _bdot(a, w_ref[...])


def combine_out(os_, lses, w, r, tm=512):
    ng = len(os_)
    m = os_[0].shape[0]
    gw = A_GW
    d = w.shape[1]
    tm = _row_tile(m, tm)
    slabs = [pl.BlockSpec((tm, gw), lambda i, c=(g if o.shape[1] > gw else 0): (i, c))
             for g, o in enumerate(os_)]
    return pl.pallas_call(
        functools.partial(_combine_out_kernel, ng=ng),
        grid=(m // tm,),
        in_specs=slabs * 2 + [pl.BlockSpec((ng * gw, d), lambda i: (0, 0)),
                              pl.BlockSpec((tm, d), lambda i: (i, 0))],
        out_specs=pl.BlockSpec((tm, d), lambda i: (i, 0)),
        out_shape=jax.ShapeDtypeStruct((m, d), F32),
        compiler_params=_cparams("parallel"),
        name="combine_out",
    )(*os_, *lses, w, r)


def _split3_dot(a01, b):
    a = a01.astype(BF16)
    b1 = b.astype(BF16)
    r1 = b - b1.astype(F32)
    b2 = r1.astype(BF16)
    b3 = (r1 - b2.astype(F32)).astype(BF16)
    dot = functools.partial(jnp.dot, preferred_element_type=F32)
    return dot(a, b1) + dot(a, b2) + dot(a, b3)


def _delta_kernel(q_ref, k_ref, v_ref, z_ref, gt_ref, hq_ref, hk_ref, hv_ref, cw_q, cw_k, cw_v,
                  alog_ref, dtb_ref, nw_ref, s0_ref, o_ref, s_ref, xq, xk, xv,
                  *, t_len, t_valid, chunk, cg, hb, n_heads):
    halo = 8
    for src, hal, dst in ((q_ref, hq_ref, xq), (k_ref, hk_ref, xk), (v_ref, hv_ref, xv)):
        dst[0:halo, :] = hal[0]
        dst[halo:, :] = src[0]
    rows = cg * chunk
    ci = _iota((chunk, chunk), 0)
    si = _iota((chunk, chunk), 1)
    ltri = (ci >= si).astype(F32)
    strict = ci > si
    eye = (ci == si).astype(F32)
    merge_masks = []
    for lg in range(chunk.bit_length() - 1):
        merge_masks.append(((ci >> (lg + 1)) == (si >> (lg + 1)))
                           & (((ci >> lg) & 1) == 1) & (((si >> lg) & 1) == 0))
    lane_r = _iota((rows, LANES), 1)
    lane_1 = _iota((1, LANES), 1)
    nw = nw_ref[...]
    head_base = pl.program_id(1) * hb

    def conv_act(xs, cw, r0, cols):
        win = xs[pl.ds(r0, rows + halo), cols]
        w = cw[:, cols]
        c = win[halo - (CONV_W - 1):halo - (CONV_W - 1) + rows, :] * w[0:1, :]
        for j in range(1, CONV_W):
            off = halo - (CONV_W - 1) + j
            c = c + win[off:off + rows, :] * w[j:j + 1, :]
        return _silu(c)

    def l2n(x):
        return x * lax.rsqrt(jnp.sum(x * x, axis=-1, keepdims=True) + L2_EPS)

    def group(i, states):
        r0 = pl.multiple_of(i * rows, rows)
        gate = gt_ref[0, pl.ds(r0, rows), :]
        new_states = []
        for hh in range(hb):
            cols = slice(hh * DK_B, (hh + 1) * DK_B)
            head = head_base + hh
            q_all = l2n(conv_act(xq, cw_q, r0, cols)) * (DK_B ** -0.5)
            k_all = l2n(conv_act(xk, cw_k, r0, cols))
            v_all = conv_act(xv, cw_v, r0, cols)
            a_col = jnp.sum(jnp.where(lane_r == head, gate, 0.0), axis=-1, keepdims=True)
            b_col = jnp.sum(jnp.where(lane_r == n_heads + head, gate, 0.0), axis=-1, keepdims=True)
            a_log = jnp.sum(jnp.where(lane_1 == head, alog_ref[...], 0.0), axis=-1, keepdims=True)
            dtb = jnp.sum(jnp.where(lane_1 == head, dtb_ref[...], 0.0), axis=-1, keepdims=True)
            x = a_col + dtb
            softplus = jnp.maximum(x, 0.0) + jnp.log(1.0 + jnp.exp(-jnp.abs(x)))
            g_col = -jnp.exp(a_log) * softplus
            beta_col = 1.0 / (1.0 + jnp.exp(-b_col))
            if t_valid < t_len:
                valid = (r0 + _iota((rows, 1), 0)) < t_valid
                k_all = jnp.where(valid, k_all, 0.0)
                g_col = jnp.where(valid, g_col, 0.0)
                beta_col = jnp.where(valid, beta_col, 0.0)
            z_all = z_ref[0, pl.ds(r0, rows), cols]
            s_state = states[hh]
            for c in range(cg):
                rs = slice(c * chunk, (c + 1) * chunk)
                q, k, v = q_all[rs], k_all[rs], v_all[rs]
                beta = beta_col[rs]
                g_b = jnp.broadcast_to(g_col[rs], (chunk, LANES))
                g_cum = _split3_dot(ltri, g_b)
                diff = _split3_dot(ltri, jnp.where(strict, g_b[:, :chunk], 0.0))
                dec = jnp.where(ci >= si, jnp.exp(jnp.where(ci >= si, diff, 0.0)), 0.0)
                kb = k * beta
                a_mat = jnp.where(strict, _bdot_nt(kb, k) * dec, 0.0)
                t_inv = eye - jnp.where(merge_masks[0], a_mat, 0.0)
                for mask in merge_masks[1:]:
                    t_inv = t_inv - _bdot(t_inv, _bdot(jnp.where(mask, a_mat, 0.0), t_inv))
                e_g = jnp.exp(g_cum)
                uw = _bdot(t_inv, jnp.concatenate([v * beta, kb * e_g], axis=1))
                u, w = uw[:, :DK_B], uw[:, DK_B:]
                attn = _bdot_nt(q, k) * dec
                qe = q * e_g
                g_last = g_cum[chunk - 1:chunk, :]
                kd = k * jnp.exp(g_last - g_cum)
                v_new = u - _bdot(w, s_state)
                o = _bdot(qe, s_state) + _bdot(attn, v_new)
                s_state = s_state * jnp.exp(g_last) + _bdot_tn(kd, v_new)
                o = o * lax.rsqrt(jnp.mean(o * o, axis=-1, keepdims=True) + RMS_EPS) * nw
                o_ref[0, pl.ds(r0 + c * chunk, chunk), cols] = o * _silu(z_all[rs])
            new_states.append(s_state)
        return tuple(new_states)

    init = tuple(s0_ref[0, hh] for hh in range(hb))
    final = lax.fori_loop(0, t_len // rows, group, init)
    for hh in range(hb):
        s_ref[0, hh] = final[hh]


def delta_core(proj, conv_halo, s0, conv_w, a_log, dt_bias, norm_w, *, t_valid, chunk, cg, hb):
    b, t_len, width = proj.shape
    n_heads = s0.shape[1]
    hw = n_heads * DK_B
    assert width == 4 * hw + LANES and t_len % (cg * chunk) == 0 and n_heads % hb == 0
    nhb = n_heads // hb
    bw = hb * DK_B

    def col_spec(rows, c):
        return pl.BlockSpec((1, rows, bw), lambda i, h: (i, 0, c * nhb + h))

    def row_vec(x):
        return jnp.pad(x.astype(F32), (0, LANES - x.shape[0])).reshape(1, LANES)

    vec_spec = pl.BlockSpec((1, LANES), lambda i, h: (0, 0))
    cw_specs = [pl.BlockSpec((CONV_W, bw), lambda i, h, c=c: (0, c * nhb + h)) for c in range(3)]
    return pl.pallas_call(
        functools.partial(_delta_kernel, t_len=t_len, t_valid=t_valid, chunk=chunk, cg=cg,
                          hb=hb, n_heads=n_heads),
        grid=(b, nhb),
        in_specs=[col_spec(t_len, 0), col_spec(t_len, 1), col_spec(t_len, 2), col_spec(t_len, 3),
                  pl.BlockSpec((1, t_len, LANES), lambda i, h: (i, 0, 4 * hw // LANES)),
                  col_spec(8, 0), col_spec(8, 1), col_spec(8, 2)] + cw_specs
                 + [vec_spec, vec_spec, vec_spec,
                    pl.BlockSpec((1, hb, DK_B, DK_B), lambda i, h: (i, h, 0, 0))],
        out_specs=[pl.BlockSpec((1, t_len, bw), lambda i, h: (i, 0, h)),
                   pl.BlockSpec((1, hb, DK_B, DK_B), lambda i, h: (i, h, 0, 0))],
        out_shape=[jax.ShapeDtypeStruct((b, t_len, hw), F32),
                   jax.ShapeDtypeStruct(s0.shape, F32)],
        scratch_shapes=[pltpu.VMEM((t_len + 8, bw), F32)] * 3,
        compiler_params=_cparams("parallel", "parallel"),
        name="delta_core",
    )(proj, proj, proj, proj, proj, conv_halo, conv_halo, conv_halo, conv_w, conv_w, conv_w,
      row_vec(a_log), row_vec(dt_bias), norm_w.astype(F32).reshape(1, DK_B), s0)


def _pool_prompt_kernel(x_ref, xh_ref, g_ref, w_ref, sc_ref, o_ref, tail_ref, hs, *, tm):
    i = pl.program_id(1)
    g = g_ref[...]
    x = x_ref[0]
    h = _rms(x, g)
    hs[0:16, :] = jnp.where(i > 0, _rms(xh_ref[0], g), 0.0)
    hs[16:, :] = h
    tail_ref[0] = h[tm - 16:, :]
    pos = (i * tm + _iota((tm, 1), 0) + 1).astype(F32)
    cg = x.shape[1] // len(POOL_WINDOWS)
    for gi, w in enumerate(POOL_WINDOWS):
        cols = slice(gi * cg, (gi + 1) * cg)
        tot = h[:, cols]
        for j in range(1, w):
            tot = tot + hs[16 - j:16 - j + tm, cols]
        d = tot / jnp.minimum(float(w), pos) - h[:, cols]
        o_ref[0, :, cols] = x[:, cols] + _bdot(d, w_ref[gi]) * sc_ref[:, cols]


def pool_prompt(x, g, w_grp, scale, tm=512):
    b, s, d = x.shape
    tm = _row_tile(s, tm)
    per = tm // 16
    ng, cgw, _ = w_grp.shape
    return pl.pallas_call(
        functools.partial(_pool_prompt_kernel, tm=tm),
        grid=(b, s // tm),
        in_specs=[pl.BlockSpec((1, tm, d), lambda i, j: (i, j, 0)),
                  pl.BlockSpec((1, 16, d), lambda i, j: (i, jnp.maximum(j * per - 1, 0), 0)),
                  pl.BlockSpec((1, d), lambda i, j: (0, 0)),
                  pl.BlockSpec((ng, cgw, cgw), lambda i, j: (0, 0, 0)),
                  pl.BlockSpec((1, d), lambda i, j: (0, 0))],
        out_specs=[pl.BlockSpec((1, tm, d), lambda i, j: (i, j, 0)),
                   pl.BlockSpec((1, 16, d), lambda i, j: (i, 0, 0))],
        out_shape=[jax.ShapeDtypeStruct((b, s, d), F32),
                   jax.ShapeDtypeStruct((b, 16, d), F32)],
        scratch_shapes=[pltpu.VMEM((tm + 16, d), F32)],
        compiler_params=_cparams("parallel", "arbitrary"),
        name="pool_prompt",
    )(x, x, g.reshape(1, d), w_grp, scale.reshape(1, d))


def _pool_sample_kernel(x_ref, pre_ref, g_ref, w_ref, sc_ref, o_ref, h_ref, *, t_new, n_pre):
    g = g_ref[...]
    xs = [x_ref[t] for t in range(t_new)]
    rows = [pre_ref[r] for r in range(n_pre)] + [_rms(x, g) for x in xs]
    cg = xs[0].shape[1] // len(POOL_WINDOWS)
    for t in range(t_new):
        h_ref[t] = rows[n_pre + t]
        for gi, w in enumerate(POOL_WINDOWS):
            cols = slice(gi * cg, (gi + 1) * cg)
            lo = max(n_pre + t - w + 1, 0)
            tot = rows[lo][:, cols]
            for r in range(lo + 1, n_pre + t + 1):
                tot = tot + rows[r][:, cols]
            d = tot / float(min(w, t + 1 + n_pre)) - rows[n_pre + t][:, cols]
            o_ref[t, :, cols] = xs[t][:, cols] + _bdot(d, w_ref[gi]) * sc_ref[:, cols]


def pool_sample(x_t, pre_t, g, w_grp, scale):
    t_new, bd_, d = x_t.shape
    n_pre = pre_t.shape[0]
    ng, cgw, _ = w_grp.shape
    full = lambda shape: pl.BlockSpec(shape, lambda i: (0,) * len(shape))
    return pl.pallas_call(
        functools.partial(_pool_sample_kernel, t_new=t_new, n_pre=n_pre),
        grid=(1,),
        in_specs=[full((t_new, bd_, d)), full((n_pre, bd_, d)), full((1, d)),
                  full((ng, cgw, cgw)), full((1, d))],
        out_specs=[full((t_new, bd_, d)), full((t_new, bd_, d))],
        out_shape=[jax.ShapeDtypeStruct((t_new, bd_, d), F32),
                   jax.ShapeDtypeStruct((t_new, bd_, d), F32)],
        compiler_params=_cparams("arbitrary"),
        name="pool_sample",
    )(x_t, pre_t, g.reshape(1, d), w_grp, scale.reshape(1, d))


def _mixer_a(xp, xs, g1, w_in, w_out, caches, tabs_p, tabs_s):
    bp, sp, d = xp.shape
    bs, ts, _ = xs.shape
    xp2, xs2 = xp.reshape(-1, d), xs.reshape(-1, d)
    qkv_p = norm_mm(xp2, g1, w_in).reshape(bp, sp, -1)
    qkv_s = norm_mm(xs2, g1, w_in).reshape(bs, ts, -1)
    os_, lses, kv_p = [], [], []
    for g, (window, dil) in enumerate(A_GROUPS):
        assert sp >= window
        o, lse, kv = attn_prompt(qkv_p, g, dil, *tabs_p)
        os_.append(o.reshape(-1, A_GW))
        lses.append(lse.reshape(-1, A_GW))
        kv_p.append(kv.reshape(bp, window, 2, HG_A, HD_A))
    yp = combine_out(os_, lses, w_out, xp2).reshape(xp.shape)
    o_s, lse_s, new_rows = attn_sample(qkv_s, *caches, *tabs_s)
    o_s, lse_s = o_s.reshape(bs * ts, -1), lse_s.reshape(bs * ts, -1)
    ys = combine_out([o_s] * 3, [lse_s] * 3, w_out, xs2).reshape(xs.shape)
    kv_s = []
    for g, cache in enumerate(caches):
        rows = new_rows[:, :, g * 2 * A_GW:(g + 1) * 2 * A_GW].reshape(bs, ts, 2, HG_A, HD_A)
        kv_s.append(jnp.concatenate([cache[:, ts:], rows], axis=1))
    return yp, ys, kv_p, kv_s


def _mixer_b(xp, xs, g1, w_in, conv_w, a_log, dt_bias, norm_w, w_out, conv_s0, state_s0):
    bp, sp, d = xp.shape
    bs, ts, _ = xs.shape
    n_heads = state_s0.shape[1]
    qkv_w = 3 * n_heads * DK_B
    xp2, xs2 = xp.reshape(-1, d), xs.reshape(-1, d)
    proj_p = norm_mm(xp2, g1, w_in).reshape(bp, sp, -1)
    proj_s = norm_mm(xs2, g1, w_in).reshape(bs, ts, -1)
    consts = (conv_w, a_log, dt_bias, norm_w)
    o_p, st_p = delta_core(proj_p, jnp.zeros((bp, 8, qkv_w), F32),
                           jnp.zeros((bp,) + state_s0.shape[1:], F32), *consts,
                           t_valid=sp, chunk=CHUNK_B, cg=4, hb=2)
    yp = mm_res(o_p.reshape(bp * sp, -1), w_out, xp2).reshape(xp.shape)
    t_pad = 8
    assert CONV_W - 1 <= ts <= t_pad
    o_s, st_s = delta_core(jnp.pad(proj_s, ((0, 0), (0, t_pad - ts), (0, 0))),
                           jnp.pad(conv_s0, ((0, 0), (8 - (CONV_W - 1), 0), (0, 0))),
                           state_s0, *consts, t_valid=ts, chunk=t_pad, cg=1, hb=n_heads)
    ys = mm_res(o_s[:, :ts].reshape(bs * ts, -1), w_out, xs2).reshape(xs.shape)
    conv_p = proj_p[:, sp - (CONV_W - 1):, :qkv_w]
    conv_s = proj_s[:, ts - (CONV_W - 1):, :qkv_w]
    return yp, ys, (st_p, conv_p), (st_s, conv_s)


def _mixer_c(xp, xs, g1, w_grp, scale, pool_s0):
    ts = xs.shape[1]
    yp, tail = pool_prompt(xp, g1, w_grp, scale)
    pool_p = tail[:, tail.shape[1] - POOL_BUF:]
    y_t, h_t = pool_sample(xs.transpose(1, 0, 2), pool_s0.transpose(1, 0, 2), g1, w_grp, scale)
    pool_s = jnp.concatenate([pool_s0, h_t.transpose(1, 0, 2)], axis=1)[:, ts:]
    return yp, y_t.transpose(1, 0, 2), pool_p, pool_s


def kernel(x_prompt, x_sample, cache_a_kv1, cache_a_kv2, cache_a_kv3, state_b_s, state_b_conv, state_c_pool, norm1, norm2, norm_f, a_w_in, a_w_out, b_w_in, b_conv, b_a_log, b_dt_bias, b_norm, b_w_out, c_w, c_scale, mlp_w1, mlp_w2):
    xp, xs = x_prompt, x_sample
    depth = norm1.shape[0]
    d = xp.shape[-1]
    past_len = cache_a_kv3.shape[2]
    tabs_p = _rope_tables(jnp.arange(xp.shape[1]))
    tabs_s = _rope_tables(past_len + jnp.arange(xs.shape[1]))
    b_pad = (-b_w_in.shape[2]) % LANES
    a_p, a_s, b_p, b_s, c_p, c_s = [], [], [], [], [], []
    for i in range(depth):
        kind, j = i % 3, i // 3
        if kind == 0:
            caches = (cache_a_kv1[j], cache_a_kv2[j], cache_a_kv3[j])
            xp, xs, kv_p, kv_s = _mixer_a(xp, xs, norm1[i], a_w_in[j].astype(BF16),
                                          a_w_out[j].astype(BF16), caches, tabs_p, tabs_s)
            a_p.append(kv_p)
            a_s.append(kv_s)
        elif kind == 1:
            w_in = jnp.pad(b_w_in[j], ((0, 0), (0, b_pad))).astype(BF16)
            xp, xs, st_p, st_s = _mixer_b(xp, xs, norm1[i], w_in, b_conv[j], b_a_log[j], b_dt_bias[j],
                                          b_norm[j], b_w_out[j].astype(BF16), state_b_conv[j], state_b_s[j])
            b_p.append(st_p)
            b_s.append(st_s)
        else:
            xp, xs, pool_p, pool_s = _mixer_c(xp, xs, norm1[i], c_w[j].astype(BF16), c_scale[j],
                                              state_c_pool[j])
            c_p.append(pool_p)
            c_s.append(pool_s)
        w1, w2 = mlp_w1[i].astype(BF16), mlp_w2[i].astype(BF16)
        last = i == depth - 1
        xp = mlp_res(xp.reshape(-1, d), norm2[i], w1, w2, norm_f, last).reshape(xp.shape)
        xs = mlp_res(xs.reshape(-1, d), norm2[i], w1, w2, norm_f, last).reshape(xs.shape)
    stack = lambda items, k: jnp.stack([e[k] for e in items])
    return (xp, xs,
            stack(a_p, 0), stack(a_p, 1), stack(a_p, 2),
            stack(b_p, 0), stack(b_p, 1), jnp.stack(c_p),
            stack(a_s, 0), stack(a_s, 1), stack(a_s, 2),
            stack(b_s, 0), stack(b_s, 1), jnp.stack(c_s))
```

```python
import functools
import math

import numpy as np
import jax
import jax.numpy as jnp
from jax import lax
from jax.experimental import pallas as pl
from jax.experimental.pallas import tpu as pltpu

F32 = jnp.float32
BF16 = jnp.bfloat16

RMS_EPS = 1e-6
L2_EPS = 1e-6
ROPE_THETA = 10000.0
A_GROUPS = ((128, 1), (512, 4), (2048, 16))
HG_A = 6
HD_A = 64
A_GW = HG_A * HD_A
A_BLK = 128
CONV_W = 4
DK_B = 128
CHUNK_B = 64
POOL_WINDOWS = (2, 4, 8, 16)
POOL_BUF = max(POOL_WINDOWS) - 1
LANES = 128
VMEM_LIMIT = 56 * 1024 * 1024
NEG_INF = float("-inf")


def _cparams(*sem):
    return pltpu.CompilerParams(dimension_semantics=sem, vmem_limit_bytes=VMEM_LIMIT)


def _bdot(a, b):
    return jnp.dot(a.astype(BF16), b.astype(BF16), preferred_element_type=F32)


def _bdot_nt(a, b):
    return lax.dot_general(a.astype(BF16), b.astype(BF16), (((1,), (1,)), ((), ())),
                           preferred_element_type=F32)


def _bdot_tn(a, b):
    return lax.dot_general(a.astype(BF16), b.astype(BF16), (((0,), (0,)), ((), ())),
                           preferred_element_type=F32)


def _rms(x, g):
    return x * lax.rsqrt(jnp.mean(x * x, axis=-1, keepdims=True) + RMS_EPS) * g


def _silu(x):
    return x * (1.0 / (1.0 + jnp.exp(-x)))


def _iota(shape, dim):
    return lax.broadcasted_iota(jnp.int32, shape, dim)


def _row_tile(m, want):
    t = min(m, want)
    assert m % t == 0, (m, t)
    return t


def _norm_mm_kernel(x_ref, g_ref, w_ref, o_ref):
    h = _rms(x_ref[...], g_ref[...])
    o_ref[...] = _bdot(h, w_ref[...])


def norm_mm(x, g, w, tm=512):
    m, d = x.shape
    n = w.shape[1]
    tm = _row_tile(m, tm)
    return pl.pallas_call(
        _norm_mm_kernel,
        grid=(m // tm,),
        in_specs=[pl.BlockSpec((tm, d), lambda i: (i, 0)),
                  pl.BlockSpec((1, d), lambda i: (0, 0)),
                  pl.BlockSpec((d, n), lambda i: (0, 0))],
        out_specs=pl.BlockSpec((tm, n), lambda i: (i, 0)),
        out_shape=jax.ShapeDtypeStruct((m, n), F32),
        compiler_params=_cparams("parallel"),
        name="norm_mm",
    )(x, g.reshape(1, d), w)


def _mm_res_kernel(a_ref, w_ref, r_ref, o_ref):
    o_ref[...] = r_ref[...] + _bdot(a_ref[...], w_ref[...])


def mm_res(a, w, r, tm=512):
    m, k = a.shape
    n = w.shape[1]
    tm = _row_tile(m, tm)
    return pl.pallas_call(
        _mm_res_kernel,
        grid=(m // tm,),
        in_specs=[pl.BlockSpec((tm, k), lambda i: (i, 0)),
                  pl.BlockSpec((k, n), lambda i: (0, 0)),
                  pl.BlockSpec((tm, n), lambda i: (i, 0))],
        out_specs=pl.BlockSpec((tm, n), lambda i: (i, 0)),
        out_shape=jax.ShapeDtypeStruct((m, n), F32),
        compiler_params=_cparams("parallel"),
        name="mm_res",
    )(a, w, r)


def _mlp_kernel(x_ref, g_ref, w1_ref, w2_ref, gf_ref, o_ref, *, fc, final_norm):
    x = x_ref[...]
    h = _rms(x, g_ref[...]).astype(BF16)
    acc = x
    for c in range(w1_ref.shape[1] // fc):
        a = jnp.dot(h, w1_ref[:, c * fc:(c + 1) * fc], preferred_element_type=F32)
        a = jnp.square(jnp.maximum(a, 0.0))
        acc = acc + _bdot(a, w2_ref[c * fc:(c + 1) * fc, :])
    if final_norm:
        acc = _rms(acc, gf_ref[...])
    o_ref[...] = acc


def mlp_res(x, g, w1, w2, gf, final_norm, tm=512, fc=1024):
    m, d = x.shape
    f = w1.shape[1]
    tm = _row_tile(m, tm)
    return pl.pallas_call(
        functools.partial(_mlp_kernel, fc=fc, final_norm=final_norm),
        grid=(m // tm,),
        in_specs=[pl.BlockSpec((tm, d), lambda i: (i, 0)),
                  pl.BlockSpec((1, d), lambda i: (0, 0)),
                  pl.BlockSpec((d, f), lambda i: (0, 0)),
                  pl.BlockSpec((f, d), lambda i: (0, 0)),
                  pl.BlockSpec((1, d), lambda i: (0, 0))],
        out_specs=pl.BlockSpec((tm, d), lambda i: (i, 0)),
        out_shape=jax.ShapeDtypeStruct((m, d), F32),
        compiler_params=_cparams("parallel"),
        name="mlp_res",
    )(x, g.reshape(1, d), w1, w2, gf.reshape(1, d))


def _rope_tables(pos):
    half = HD_A // 2
    inv = ROPE_THETA ** (-jnp.arange(half, dtype=F32) / half)
    ang = pos.astype(F32)[:, None] * inv[None, :]
    cos = jnp.tile(jnp.cos(ang), (1, LANES // half))
    sin = jnp.sin(ang)
    sin = jnp.tile(jnp.concatenate([-sin, sin], axis=1), (1, LANES // HD_A))
    return cos, sin


def _rope_slab(x, cos, sin):
    half = HD_A // 2
    first = (_iota(x.shape, 1) % HD_A) < half
    swapped = jnp.where(first, pltpu.roll(x, LANES - half, 1), pltpu.roll(x, half, 1))
    return x * cos + swapped * sin


def _attn_prompt_kernel(q_ref, k_ref, v_ref, cos_ref, sin_ref, o_ref, lse_ref, kv_ref,
                        qs, ks, vs, *, n):
    nb = n // A_BLK
    cos = cos_ref[...]
    sin = sin_ref[...]
    lane = _iota((A_BLK, LANES), 1)
    head0 = lane < HD_A
    qi = _iota((A_BLK, 2 * A_BLK), 0)
    kj = _iota((A_BLK, 2 * A_BLK), 1)
    dist = A_BLK + qi - kj
    band = (dist >= 0) & (dist <= A_BLK)
    cur = kj >= A_BLK
    scale = HD_A ** -0.5
    ks[0:A_BLK, :] = jnp.zeros((A_BLK, LANES), BF16)
    vs[0:A_BLK, :] = jnp.zeros((A_BLK, LANES), BF16)
    for p in range(A_GW // LANES):
        cols = slice(p * LANES, (p + 1) * LANES)
        k2 = _rope_slab(k_ref[0, :, cols], cos, sin)
        v2 = v_ref[0, :, cols]
        kv_ref[0, :, cols] = k2[n - A_BLK:, :]
        kv_ref[0, :, A_GW + p * LANES:A_GW + (p + 1) * LANES] = v2[n - A_BLK:, :]
        qs[...] = (_rope_slab(q_ref[0, :, cols], cos, sin) * scale).astype(BF16)
        ks[A_BLK:, :] = k2.astype(BF16)
        vs[A_BLK:, :] = v2.astype(BF16)

        def block(i, carry):
            r0 = pl.multiple_of(i * A_BLK, A_BLK)
            q2 = qs[pl.ds(r0, A_BLK), :]
            kc = ks[pl.ds(r0, 2 * A_BLK), :]
            vc = vs[pl.ds(r0, 2 * A_BLK), :]
            mask = band & (cur | (i > 0))
            outs = []
            for hsel in (head0, ~head0):
                s = _bdot_nt(jnp.where(hsel, q2, jnp.zeros_like(q2)), kc)
                s = jnp.where(mask, s, NEG_INF)
                m = jnp.max(s, axis=-1, keepdims=True)
                e = jnp.exp(s - m)
                l = jnp.sum(e, axis=-1, keepdims=True)
                outs.append((_bdot(e, vc) / l, m + jnp.log(l)))
            o_ref[0, pl.ds(r0, A_BLK), cols] = jnp.where(head0, outs[0][0], outs[1][0])
            lse_ref[0, pl.ds(r0, A_BLK), cols] = jnp.where(
                head0, jnp.broadcast_to(outs[0][1], (A_BLK, LANES)),
                jnp.broadcast_to(outs[1][1], (A_BLK, LANES)))
            return carry

        lax.fori_loop(0, nb, block, 0)


def attn_prompt(qkv, g, dil, cos, sin):
    b, s, width = qkv.shape
    n = s // dil
    assert n * dil == s and n % A_BLK == 0
    qkv_v = qkv.reshape(b, n, dil * width)
    cos_v = cos.reshape(n, dil * LANES)
    sin_v = sin.reshape(n, dil * LANES)
    slabs = width // A_GW

    def col_spec(c):
        return pl.BlockSpec((1, n, A_GW), lambda i, r: (i, 0, r * slabs + g * 3 + c))

    tab_spec = pl.BlockSpec((n, LANES), lambda i, r: (0, r))
    out_spec = pl.BlockSpec((1, n, A_GW), lambda i, r: (i, 0, r))
    o, lse, kv = pl.pallas_call(
        functools.partial(_attn_prompt_kernel, n=n),
        grid=(b, dil),
        in_specs=[col_spec(0), col_spec(1), col_spec(2), tab_spec, tab_spec],
        out_specs=[out_spec, out_spec,
                   pl.BlockSpec((1, A_BLK, 2 * A_GW), lambda i, r: (i, 0, r))],
        out_shape=[jax.ShapeDtypeStruct((b, n, dil * A_GW), F32),
                   jax.ShapeDtypeStruct((b, n, dil * A_GW), F32),
                   jax.ShapeDtypeStruct((b, A_BLK, dil * 2 * A_GW), F32)],
        scratch_shapes=[pltpu.VMEM((n, LANES), BF16),
                        pltpu.VMEM((n + A_BLK, LANES), BF16),
                        pltpu.VMEM((n + A_BLK, LANES), BF16)],
        compiler_params=_cparams("parallel", "parallel"),
        name=f"attn_prompt_g{g}",
    )(qkv_v, qkv_v, qkv_v, cos_v, sin_v)
    return o.reshape(b, s, A_GW), lse.reshape(b, s, A_GW), kv


def _norm_mm_t_kernel(wt_ref, x_ref, g_ref, o_ref):
    h = _rms(x_ref[...], g_ref[...])
    o_ref[...] = _bdot_nt(wt_ref[...], h)


def norm_mm_t(wt, x, g, tn=768):
    n, d = wt.shape
    m = x.shape[0]
    tn = _row_tile(n, tn)
    return pl.pallas_call(
        _norm_mm_t_kernel,
        grid=(n // tn,),
        in_specs=[pl.BlockSpec((tn, d), lambda i: (i, 0)),
                  pl.BlockSpec((m, d), lambda i: (0, 0)),
                  pl.BlockSpec((1, d), lambda i: (0, 0))],
        out_specs=pl.BlockSpec((tn, m), lambda i: (i, 0)),
        out_shape=jax.ShapeDtypeStruct((n, m), F32),
        compiler_params=_cparams("parallel"),
        name="norm_mm_t",
    )(wt, x, g.reshape(1, d))


def _attn_sample_kernel(*refs, t_new, length, window, dil, aliased):
    if aliased:
        refs = refs[1:]
    (q_ref, kt_ref, vt_ref, cos_ref, sin_ref, cost_ref, sint_ref, cache_ref,
     o_ref, lse_ref, out_ref) = refs
    b = pl.program_id(0)
    per_blk = LANES // t_new
    new0 = LANES - t_new
    shift = new0 - (b % per_blk) * t_new
    half = HD_A // 2
    ncol = length // LANES
    scale = HD_A ** -0.5
    cos, sin = cos_ref[...], sin_ref[...]
    cos_t, sin_t = cost_ref[...], sint_ref[...]
    q = jnp.concatenate(
        [_rope_slab(q_ref[0, :, p * LANES:(p + 1) * LANES], cos, sin) for p in range(A_GW // LANES)],
        axis=1) * scale
    q8 = jnp.concatenate([q, jnp.zeros((8 - t_new, A_GW), F32)], axis=0)
    lane = _iota((HD_A, LANES), 1)
    is_new = lane >= new0
    t_b = _iota((8, length), 0)
    l_b = _iota((8, length), 1)
    valid_b = ((l_b & (dil - 1)) == (t_b & (dil - 1))) & (l_b >= t_b + (length - window))
    t_n = _iota((8, LANES), 0)
    u_n = _iota((8, LANES), 1) - new0
    valid_n = (u_n >= 0) & (u_n <= t_n) & (((t_n - u_n) & (dil - 1)) == 0)
    o_parts, lse_parts = [], []
    for h in range(HG_A):
        hr = slice(h * HD_A, (h + 1) * HD_A)
        kt = kt_ref[hr, :]
        x1, x2 = kt[:half], kt[half:]
        kt = jnp.concatenate([x1 * cos_t - x2 * sin_t, x2 * cos_t + x1 * sin_t], axis=0)
        k_new = jnp.where(is_new, pltpu.roll(kt, shift, 1), 0.0)
        v_new = jnp.where(is_new, pltpu.roll(vt_ref[hr, :], shift, 1), 0.0)
        k_buf = cache_ref[0, 0, 0, h]
        v_buf = cache_ref[0, 0, 1, h]
        q_h = q8[:, hr]
        s_b = jnp.where(valid_b, _bdot(q_h, k_buf), NEG_INF)
        s_n = jnp.where(valid_n, _bdot(q_h, k_new), NEG_INF)
        m = jnp.maximum(jnp.max(s_b, axis=-1, keepdims=True), jnp.max(s_n, axis=-1, keepdims=True))
        e_b = jnp.exp(s_b - m)
        e_n = jnp.exp(s_n - m)
        l = jnp.sum(e_b, axis=-1, keepdims=True) + jnp.sum(e_n, axis=-1, keepdims=True)
        o_parts.append((_bdot_nt(e_b, v_buf) + _bdot_nt(e_n, v_new)) / l)
        lse_parts.append(jnp.broadcast_to(m + jnp.log(l), (8, HD_A)))
        for c, (buf, new) in enumerate(((k_buf, k_new), (v_buf, v_new))):
            rolled = [pltpu.roll(buf[:, i * LANES:(i + 1) * LANES], new0, 1) for i in range(ncol)]
            for i in range(ncol):
                nxt = rolled[i + 1] if i + 1 < ncol else new
                out_ref[0, 0, c, h, :, i * LANES:(i + 1) * LANES] = jnp.where(is_new, nxt, rolled[i])
    o_ref[0] = jnp.concatenate(o_parts, axis=1)[:t_new]
    lse_ref[0] = jnp.concatenate(lse_parts, axis=1)[:t_new]


def attn_sample(g, layer, qkv, kvt_new, tabs, tabs_t, cache_t, prev_out):
    window, dil = A_GROUPS[g]
    bd_, t_new, width = qkv.shape
    n_layers, _, _, _, _, length = cache_t.shape
    assert LANES % t_new == 0 and length % LANES == 0 and t_new <= 8 and dil & (dil - 1) == 0
    per_blk = LANES // t_new
    slabs = width // A_GW
    kv_rows = kvt_new.shape[0] // A_GW
    aliased = prev_out is not None
    in_specs = [pl.BlockSpec((1, t_new, A_GW), lambda i: (i, 0, g * 3)),
                pl.BlockSpec((A_GW, LANES), lambda i: (2 * g, i // per_blk)),
                pl.BlockSpec((A_GW, LANES), lambda i: (2 * g + 1, i // per_blk)),
                pl.BlockSpec((t_new, LANES), lambda i: (0, 0)),
                pl.BlockSpec((t_new, LANES), lambda i: (0, 0)),
                pl.BlockSpec((HD_A // 2, LANES), lambda i: (0, 0)),
                pl.BlockSpec((HD_A // 2, LANES), lambda i: (0, 0)),
                pl.BlockSpec((1, 1, 2, HG_A, HD_A, length), lambda i: (layer, i, 0, 0, 0, 0))]
    args = [qkv, kvt_new, kvt_new, *tabs, *tabs_t, cache_t]
    if aliased:
        in_specs.insert(0, pl.BlockSpec(memory_space=pl.ANY))
        args.insert(0, prev_out)
    assert slabs == 3 * len(A_GROUPS) and kv_rows == 2 * len(A_GROUPS)
    return pl.pallas_call(
        functools.partial(_attn_sample_kernel, t_new=t_new, length=length, window=window, dil=dil,
                          aliased=aliased),
        grid=(bd_,),
        in_specs=in_specs,
        out_specs=[pl.BlockSpec((1, t_new, A_GW), lambda i: (i, 0, 0)),
                   pl.BlockSpec((1, t_new, A_GW), lambda i: (i, 0, 0)),
                   pl.BlockSpec((1, 1, 2, HG_A, HD_A, length), lambda i: (layer, i, 0, 0, 0, 0))],
        out_shape=[jax.ShapeDtypeStruct((bd_, t_new, A_GW), F32),
                   jax.ShapeDtypeStruct((bd_, t_new, A_GW), F32),
                   jax.ShapeDtypeStruct(cache_t.shape, F32)],
        input_output_aliases={0: 2} if aliased else {},
        compiler_params=_cparams("arbitrary"),
        name=f"attn_sample_g{g}",
    )(*args)


def _combine_out_kernel(*refs, ng):
    o_refs, l_refs = refs[:ng], refs[ng:2 * ng]
    w_ref, r_ref, out_ref = refs[2 * ng:]
    ls = [l[...] for l in l_refs]
    m = functools.reduce(jnp.maximum, ls)
    es = [jnp.exp(l - m) for l in ls]
    inv = 1.0 / functools.reduce(jnp.add, es)
    a = jnp.concatenate([o[...] * (e * inv) for o, e in zip(o_refs, es)], axis=1)
    out_ref[...] = r_ref[...] + _bdot(a, w_ref[...])


def combine_out(os_, lses, w, r, tm=512):
    ng = len(os_)
    m = os_[0].shape[0]
    gw = A_GW
    d = w.shape[1]
    tm = _row_tile(m, tm)
    slabs = [pl.BlockSpec((tm, gw), lambda i, c=(g if o.shape[1] > gw else 0): (i, c))
             for g, o in enumerate(os_)]
    return pl.pallas_call(
        functools.partial(_combine_out_kernel, ng=ng),
        grid=(m // tm,),
        in_specs=slabs * 2 + [pl.BlockSpec((ng * gw, d), lambda i: (0, 0)),
                              pl.BlockSpec((tm, d), lambda i: (i, 0))],
        out_specs=pl.BlockSpec((tm, d), lambda i: (i, 0)),
        out_shape=jax.ShapeDtypeStruct((m, d), F32),
        compiler_params=_cparams("parallel"),
        name="combine_out",
    )(*os_, *lses, w, r)


def _split3_dot(a01, b):
    a = a01.astype(BF16)
    b1 = b.astype(BF16)
    r1 = b - b1.astype(F32)
    b2 = r1.astype(BF16)
    b3 = (r1 - b2.astype(F32)).astype(BF16)
    dot = functools.partial(jnp.dot, preferred_element_type=F32)
    return dot(a, b1) + dot(a, b2) + dot(a, b3)


def _delta_kernel(q_ref, k_ref, v_ref, z_ref, gt_ref, hq_ref, hk_ref, hv_ref, cw_q, cw_k, cw_v,
                  alog_ref, dtb_ref, nw_ref, s0_ref, o_ref, s_ref, xq, xk, xv,
                  *, t_len, t_valid, chunk, cg, hb, n_heads):
    halo = 8
    for src, hal, dst in ((q_ref, hq_ref, xq), (k_ref, hk_ref, xk), (v_ref, hv_ref, xv)):
        dst[0:halo, :] = hal[0]
        dst[halo:, :] = src[0]
    rows = cg * chunk
    ci = _iota((chunk, chunk), 0)
    si = _iota((chunk, chunk), 1)
    ltri = (ci >= si).astype(F32)
    strict = ci > si
    eye = (ci == si).astype(F32)
    merge_masks = []
    for lg in range(chunk.bit_length() - 1):
        merge_masks.append(((ci >> (lg + 1)) == (si >> (lg + 1)))
                           & (((ci >> lg) & 1) == 1) & (((si >> lg) & 1) == 0))
    lane_r = _iota((rows, LANES), 1)
    lane_c = _iota((chunk, LANES), 1)
    nw = nw_ref[...]
    head_base = pl.program_id(1) * hb

    def conv_act(xs, cw, r0, cols):
        win = xs[pl.ds(r0, rows + halo), cols]
        w = cw[:, cols]
        c = win[halo - (CONV_W - 1):halo - (CONV_W - 1) + rows, :] * w[0:1, :]
        for j in range(1, CONV_W):
            off = halo - (CONV_W - 1) + j
            c = c + win[off:off + rows, :] * w[j:j + 1, :]
        return _silu(c)

    def l2n(x):
        return x * lax.rsqrt(jnp.sum(x * x, axis=-1, keepdims=True) + L2_EPS)

    def lane_col(x, lane_ids, idx):
        return jnp.sum(jnp.where(lane_ids == idx, x, 0.0), axis=-1, keepdims=True)

    def group(i, states):
        r0 = pl.multiple_of(i * rows, rows)
        gate = gt_ref[0, pl.ds(r0, rows), :]
        x = gate + dtb_ref[...]
        softplus = jnp.maximum(x, 0.0) + jnp.log(1.0 + jnp.exp(-jnp.abs(x)))
        g_all = -jnp.exp(alog_ref[...]) * softplus
        beta_all = 1.0 / (1.0 + jnp.exp(-gate))
        if t_valid < t_len:
            valid = (r0 + _iota((rows, 1), 0)) < t_valid
            g_all = jnp.where(valid, g_all, 0.0)
            beta_all = jnp.where(valid, beta_all, 0.0)
        chunks = [slice(c * chunk, (c + 1) * chunk) for c in range(cg)]
        g_cum_all = [_split3_dot(ltri, g_all[rs]) for rs in chunks]
        heads = []
        for hh in range(hb):
            cols = slice(hh * DK_B, (hh + 1) * DK_B)
            k_all = l2n(conv_act(xk, cw_k, r0, cols))
            if t_valid < t_len:
                k_all = jnp.where(valid, k_all, 0.0)
            heads.append((l2n(conv_act(xq, cw_q, r0, cols)) * (DK_B ** -0.5), k_all,
                          conv_act(xv, cw_v, r0, cols),
                          lane_col(beta_all, lane_r, n_heads + head_base + hh)))
        units = [(hh, c) for c in range(cg) for hh in range(hb)]
        q_, k_, v_, kb_, gc_, dec_ = [], [], [], [], [], []
        for hh, c in units:
            q_all, k_all, v_all, beta_col = heads[hh]
            rs = chunks[c]
            g_c = jnp.broadcast_to(lane_col(g_cum_all[c], lane_c, head_base + hh), (chunk, LANES))
            g_r = jnp.transpose(jnp.concatenate(
                [g_c, jnp.zeros((LANES - chunk, LANES), F32)], axis=0))[:chunk, :chunk]
            diff = jnp.where(ci >= si, g_c[:, :chunk] - g_r, 0.0)
            q_.append(q_all[rs]); k_.append(k_all[rs]); v_.append(v_all[rs] * beta_col[rs])
            kb_.append(k_all[rs] * beta_col[rs]); gc_.append(g_c)
            dec_.append(jnp.where(ci >= si, jnp.exp(diff), 0.0))
        n_u = len(units)
        a_ = [jnp.where(strict, _bdot_nt(kb_[u], k_[u]) * dec_[u], 0.0) for u in range(n_u)]
        t_ = [eye - jnp.where(merge_masks[0], a_[u], 0.0) for u in range(n_u)]
        for mask in merge_masks[1:]:
            low = [_bdot(jnp.where(mask, a_[u], 0.0), t_[u]) for u in range(n_u)]
            t_ = [t_[u] - _bdot(t_[u], low[u]) for u in range(n_u)]
        eg_ = [jnp.exp(gc_[u]) for u in range(n_u)]
        uw_ = [_bdot(t_[u], jnp.concatenate([v_[u], kb_[u] * eg_[u]], axis=1)) for u in range(n_u)]
        attn_ = [_bdot_nt(q_[u], k_[u]) * dec_[u] for u in range(n_u)]
        states = list(states)
        for c in range(cg):
            us = [u for u in range(n_u) if units[u][1] == c]
            glast = {u: gc_[u][chunk - 1:chunk, :] for u in us}
            ws = {u: _bdot(uw_[u][:, DK_B:], states[units[u][0]]) for u in us}
            qs = {u: _bdot(q_[u] * eg_[u], states[units[u][0]]) for u in us}
            vn = {u: uw_[u][:, :DK_B] - ws[u] for u in us}
            for u in us:
                hh = units[u][0]
                kd = k_[u] * jnp.exp(glast[u] - gc_[u])
                o = qs[u] + _bdot(attn_[u], vn[u])
                states[hh] = states[hh] * jnp.exp(glast[u]) + _bdot_tn(kd, vn[u])
                o = o * lax.rsqrt(jnp.mean(o * o, axis=-1, keepdims=True) + RMS_EPS) * nw
                cols = slice(hh * DK_B, (hh + 1) * DK_B)
                z = z_ref[0, pl.ds(r0 + c * chunk, chunk), cols]
                o_ref[0, pl.ds(r0 + c * chunk, chunk), cols] = o * _silu(z)
        return tuple(states)

    init = tuple(s0_ref[0, hh] for hh in range(hb))
    final = lax.fori_loop(0, t_len // rows, group, init)
    for hh in range(hb):
        s_ref[0, hh] = final[hh]


def delta_core(proj, conv_halo, s0, conv_w, a_log, dt_bias, norm_w, *, t_valid, chunk, cg, hb):
    b, t_len, width = proj.shape
    n_heads = s0.shape[1]
    hw = n_heads * DK_B
    assert width == 4 * hw + LANES and t_len % (cg * chunk) == 0 and n_heads % hb == 0
    nhb = n_heads // hb
    bw = hb * DK_B

    def col_spec(rows, c):
        return pl.BlockSpec((1, rows, bw), lambda i, h: (i, 0, c * nhb + h))

    def row_vec(x):
        return jnp.pad(x.astype(F32), (0, LANES - x.shape[0])).reshape(1, LANES)

    vec_spec = pl.BlockSpec((1, LANES), lambda i, h: (0, 0))
    cw_specs = [pl.BlockSpec((CONV_W, bw), lambda i, h, c=c: (0, c * nhb + h)) for c in range(3)]
    return pl.pallas_call(
        functools.partial(_delta_kernel, t_len=t_len, t_valid=t_valid, chunk=chunk, cg=cg,
                          hb=hb, n_heads=n_heads),
        grid=(b, nhb),
        in_specs=[col_spec(t_len, 0), col_spec(t_len, 1), col_spec(t_len, 2), col_spec(t_len, 3),
                  pl.BlockSpec((1, t_len, LANES), lambda i, h: (i, 0, 4 * hw // LANES)),
                  col_spec(8, 0), col_spec(8, 1), col_spec(8, 2)] + cw_specs
                 + [vec_spec, vec_spec, vec_spec,
                    pl.BlockSpec((1, hb, DK_B, DK_B), lambda i, h: (i, h, 0, 0))],
        out_specs=[pl.BlockSpec((1, t_len, bw), lambda i, h: (i, 0, h)),
                   pl.BlockSpec((1, hb, DK_B, DK_B), lambda i, h: (i, h, 0, 0))],
        out_shape=[jax.ShapeDtypeStruct((b, t_len, hw), F32),
                   jax.ShapeDtypeStruct(s0.shape, F32)],
        scratch_shapes=[pltpu.VMEM((t_len + 8, bw), F32)] * 3,
        compiler_params=_cparams("parallel", "parallel"),
        name="delta_core",
    )(proj, proj, proj, proj, proj, conv_halo, conv_halo, conv_halo, conv_w, conv_w, conv_w,
      row_vec(a_log), row_vec(dt_bias), norm_w.astype(F32).reshape(1, DK_B), s0)


def _pool_prompt_kernel(x_ref, xh_ref, g_ref, w_ref, sc_ref, o_ref, tail_ref, hs, *, tm):
    i = pl.program_id(1)
    g = g_ref[...]
    x = x_ref[0]
    h = _rms(x, g)
    hs[0:16, :] = jnp.where(i > 0, _rms(xh_ref[0], g), 0.0)
    hs[16:, :] = h
    tail_ref[0] = h[tm - 16:, :]
    pos = (i * tm + _iota((tm, 1), 0) + 1).astype(F32)
    cg = x.shape[1] // len(POOL_WINDOWS)
    for gi, w in enumerate(POOL_WINDOWS):
        cols = slice(gi * cg, (gi + 1) * cg)
        tot = h[:, cols]
        for j in range(1, w):
            tot = tot + hs[16 - j:16 - j + tm, cols]
        d = tot / jnp.minimum(float(w), pos) - h[:, cols]
        o_ref[0, :, cols] = x[:, cols] + _bdot(d, w_ref[gi]) * sc_ref[:, cols]


def pool_prompt(x, g, w_grp, scale, tm=512):
    b, s, d = x.shape
    tm = _row_tile(s, tm)
    per = tm // 16
    ng, cgw, _ = w_grp.shape
    return pl.pallas_call(
        functools.partial(_pool_prompt_kernel, tm=tm),
        grid=(b, s // tm),
        in_specs=[pl.BlockSpec((1, tm, d), lambda i, j: (i, j, 0)),
                  pl.BlockSpec((1, 16, d), lambda i, j: (i, jnp.maximum(j * per - 1, 0), 0)),
                  pl.BlockSpec((1, d), lambda i, j: (0, 0)),
                  pl.BlockSpec((ng, cgw, cgw), lambda i, j: (0, 0, 0)),
                  pl.BlockSpec((1, d), lambda i, j: (0, 0))],
        out_specs=[pl.BlockSpec((1, tm, d), lambda i, j: (i, j, 0)),
                   pl.BlockSpec((1, 16, d), lambda i, j: (i, 0, 0))],
        out_shape=[jax.ShapeDtypeStruct((b, s, d), F32),
                   jax.ShapeDtypeStruct((b, 16, d), F32)],
        scratch_shapes=[pltpu.VMEM((tm + 16, d), F32)],
        compiler_params=_cparams("parallel", "arbitrary"),
        name="pool_prompt",
    )(x, x, g.reshape(1, d), w_grp, scale.reshape(1, d))


def _pool_sample_kernel(x_ref, pre_ref, g_ref, w_ref, sc_ref, o_ref, h_ref, *, t_new, n_pre):
    g = g_ref[...]
    xs = [x_ref[t] for t in range(t_new)]
    rows = [pre_ref[r] for r in range(n_pre)] + [_rms(x, g) for x in xs]
    cg = xs[0].shape[1] // len(POOL_WINDOWS)
    for t in range(t_new):
        h_ref[t] = rows[n_pre + t]
        for gi, w in enumerate(POOL_WINDOWS):
            cols = slice(gi * cg, (gi + 1) * cg)
            lo = max(n_pre + t - w + 1, 0)
            tot = rows[lo][:, cols]
            for r in range(lo + 1, n_pre + t + 1):
                tot = tot + rows[r][:, cols]
            d = tot / float(min(w, t + 1 + n_pre)) - rows[n_pre + t][:, cols]
            o_ref[t, :, cols] = xs[t][:, cols] + _bdot(d, w_ref[gi]) * sc_ref[:, cols]


def pool_sample(x_t, pre_t, g, w_grp, scale):
    t_new, bd_, d = x_t.shape
    n_pre = pre_t.shape[0]
    ng, cgw, _ = w_grp.shape
    full = lambda shape: pl.BlockSpec(shape, lambda i: (0,) * len(shape))
    return pl.pallas_call(
        functools.partial(_pool_sample_kernel, t_new=t_new, n_pre=n_pre),
        grid=(1,),
        in_specs=[full((t_new, bd_, d)), full((n_pre, bd_, d)), full((1, d)),
                  full((ng, cgw, cgw)), full((1, d))],
        out_specs=[full((t_new, bd_, d)), full((t_new, bd_, d))],
        out_shape=[jax.ShapeDtypeStruct((t_new, bd_, d), F32),
                   jax.ShapeDtypeStruct((t_new, bd_, d), F32)],
        compiler_params=_cparams("arbitrary"),
        name="pool_sample",
    )(x_t, pre_t, g.reshape(1, d), w_grp, scale.reshape(1, d))


def _rope_tables_t(pos, n_cols):
    half = HD_A // 2
    inv = ROPE_THETA ** (-jnp.arange(half, dtype=F32) / half)
    ang = inv[:, None] * jnp.tile(pos.astype(F32), n_cols // pos.shape[0])[None, :]
    return jnp.cos(ang), jnp.sin(ang)


def _mixer_a(xp, xs, g1, w_in, wt_kv, w_out, layer, caches_t, prev_outs, tabs_p, tabs_s, tabs_st):
    bp, sp, d = xp.shape
    bs, ts, _ = xs.shape
    xp2, xs2 = xp.reshape(-1, d), xs.reshape(-1, d)
    qkv_p = norm_mm(xp2, g1, w_in).reshape(bp, sp, -1)
    qkv_s = norm_mm(xs2, g1, w_in).reshape(bs, ts, -1)
    kvt_new = norm_mm_t(wt_kv, xs2, g1)
    os_, lses, kv_p = [], [], []
    os_s, lses_s, kv_s = [], [], []
    for g, (window, dil) in enumerate(A_GROUPS):
        assert sp >= window and caches_t[g].shape[-1] >= window
        o, lse, kv = attn_prompt(qkv_p, g, dil, *tabs_p)
        os_.append(o.reshape(-1, A_GW))
        lses.append(lse.reshape(-1, A_GW))
        kv_p.append(kv.reshape(bp, window, 2, HG_A, HD_A))
        o, lse, out = attn_sample(g, layer, qkv_s, kvt_new, tabs_s, tabs_st, caches_t[g],
                                  None if prev_outs is None else prev_outs[g])
        os_s.append(o.reshape(-1, A_GW))
        lses_s.append(lse.reshape(-1, A_GW))
        kv_s.append(out)
    yp = combine_out(os_, lses, w_out, xp2).reshape(xp.shape)
    ys = combine_out(os_s, lses_s, w_out, xs2).reshape(xs.shape)
    return yp, ys, kv_p, kv_s


def _mixer_b(xp, xs, g1, w_in, conv_w, a_log, dt_bias, norm_w, w_out, conv_s0, state_s0):
    bp, sp, d = xp.shape
    bs, ts, _ = xs.shape
    n_heads = state_s0.shape[1]
    qkv_w = 3 * n_heads * DK_B
    xp2, xs2 = xp.reshape(-1, d), xs.reshape(-1, d)
    proj_p = norm_mm(xp2, g1, w_in).reshape(bp, sp, -1)
    proj_s = norm_mm(xs2, g1, w_in).reshape(bs, ts, -1)
    consts = (conv_w, a_log, dt_bias, norm_w)
    o_p, st_p = delta_core(proj_p, jnp.zeros((bp, 8, qkv_w), F32),
                           jnp.zeros((bp,) + state_s0.shape[1:], F32), *consts,
                           t_valid=sp, chunk=CHUNK_B, cg=8, hb=2)
    yp = mm_res(o_p.reshape(bp * sp, -1), w_out, xp2).reshape(xp.shape)
    t_pad = 8
    assert CONV_W - 1 <= ts <= t_pad
    o_s, st_s = delta_core(jnp.pad(proj_s, ((0, 0), (0, t_pad - ts), (0, 0))),
                           jnp.pad(conv_s0, ((0, 0), (8 - (CONV_W - 1), 0), (0, 0))),
                           state_s0, *consts, t_valid=ts, chunk=t_pad, cg=1, hb=n_heads)
    ys = mm_res(o_s[:, :ts].reshape(bs * ts, -1), w_out, xs2).reshape(xs.shape)
    conv_p = proj_p[:, sp - (CONV_W - 1):, :qkv_w]
    conv_s = proj_s[:, ts - (CONV_W - 1):, :qkv_w]
    return yp, ys, (st_p, conv_p), (st_s, conv_s)


def _mixer_c(xp, xs, g1, w_grp, scale, pool_s0):
    ts = xs.shape[1]
    yp, tail = pool_prompt(xp, g1, w_grp, scale)
    pool_p = tail[:, tail.shape[1] - POOL_BUF:]
    y_t, h_t = pool_sample(xs.transpose(1, 0, 2), pool_s0.transpose(1, 0, 2), g1, w_grp, scale)
    pool_s = jnp.concatenate([pool_s0, h_t.transpose(1, 0, 2)], axis=1)[:, ts:]
    return yp, y_t.transpose(1, 0, 2), pool_p, pool_s


def kernel(x_prompt, x_sample, cache_a_kv1, cache_a_kv2, cache_a_kv3, state_b_s, state_b_conv, state_c_pool, norm1, norm2, norm_f, a_w_in, a_w_out, b_w_in, b_conv, b_a_log, b_dt_bias, b_norm, b_w_out, c_w, c_scale, mlp_w1, mlp_w2):
    xp, xs = x_prompt, x_sample
    depth = norm1.shape[0]
    d = xp.shape[-1]
    past_len = cache_a_kv3.shape[2]
    tabs_p = _rope_tables(jnp.arange(xp.shape[1]))
    pos_s = past_len + jnp.arange(xs.shape[1])
    tabs_s = _rope_tables(pos_s)
    tabs_st = _rope_tables_t(pos_s, LANES)
    caches_t = [jnp.transpose(c, (0, 1, 3, 4, 5, 2)) for c in (cache_a_kv1, cache_a_kv2, cache_a_kv3)]
    kv_cols = np.concatenate([np.arange(g * 3 * A_GW + A_GW, (g + 1) * 3 * A_GW)
                              for g in range(len(A_GROUPS))])
    b_pad = (-b_w_in.shape[2]) % LANES
    a_p, a_s, b_p, b_s, c_p, c_s = [], [], [], [], [], []
    kv_s = None
    for i in range(depth):
        kind, j = i % 3, i // 3
        if kind == 0:
            w_in = a_w_in[j].astype(BF16)
            xp, xs, kv_p, kv_s = _mixer_a(xp, xs, norm1[i], w_in, w_in[:, kv_cols].T,
                                          a_w_out[j].astype(BF16), j, caches_t, kv_s,
                                          tabs_p, tabs_s, tabs_st)
            a_p.append(kv_p)
        elif kind == 1:
            w_in = jnp.pad(b_w_in[j], ((0, 0), (0, b_pad))).astype(BF16)
            xp, xs, st_p, st_s = _mixer_b(xp, xs, norm1[i], w_in, b_conv[j], b_a_log[j], b_dt_bias[j],
                                          b_norm[j], b_w_out[j].astype(BF16), state_b_conv[j], state_b_s[j])
            b_p.append(st_p)
            b_s.append(st_s)
        else:
            xp, xs, pool_p, pool_s = _mixer_c(xp, xs, norm1[i], c_w[j].astype(BF16), c_scale[j],
                                              state_c_pool[j])
            c_p.append(pool_p)
            c_s.append(pool_s)
        w1, w2 = mlp_w1[i].astype(BF16), mlp_w2[i].astype(BF16)
        last = i == depth - 1
        xp = mlp_res(xp.reshape(-1, d), norm2[i], w1, w2, norm_f, last).reshape(xp.shape)
        xs = mlp_res(xs.reshape(-1, d), norm2[i], w1, w2, norm_f, last).reshape(xs.shape)
    stack = lambda items, k: jnp.stack([e[k] for e in items])
    return (xp, xs,
            stack(a_p, 0), stack(a_p, 1), stack(a_p, 2),
            stack(b_p, 0), stack(b_p, 1), jnp.stack(c_p),
            *[jnp.transpose(c, (0, 1, 5, 2, 3, 4)) for c in kv_s],
            stack(b_s, 0), stack(b_s, 1), jnp.stack(c_s))
```

```python
import functools
import math

import numpy as np
import jax
import jax.numpy as jnp
from jax import lax
from jax.experimental import pallas as pl
from jax.experimental.pallas import tpu as pltpu

F32 = jnp.float32
BF16 = jnp.bfloat16

RMS_EPS = 1e-6
L2_EPS = 1e-6
ROPE_THETA = 10000.0
A_GROUPS = ((128, 1), (512, 4), (2048, 16))
HG_A = 6
HD_A = 64
A_GW = HG_A * HD_A
A_BLK = 128
CONV_W = 4
DK_B = 128
CHUNK_B = 64
POOL_WINDOWS = (2, 4, 8, 16)
POOL_BUF = max(POOL_WINDOWS) - 1
LANES = 128
VMEM_LIMIT = 56 * 1024 * 1024
NEG_INF = float("-inf")


def _cparams(*sem):
    return pltpu.CompilerParams(dimension_semantics=sem, vmem_limit_bytes=VMEM_LIMIT)


def _bdot(a, b):
    return jnp.dot(a.astype(BF16), b.astype(BF16), preferred_element_type=F32)


def _bdot_nt(a, b):
    return lax.dot_general(a.astype(BF16), b.astype(BF16), (((1,), (1,)), ((), ())),
                           preferred_element_type=F32)


def _bdot_tn(a, b):
    return lax.dot_general(a.astype(BF16), b.astype(BF16), (((0,), (0,)), ((), ())),
                           preferred_element_type=F32)


def _rms(x, g):
    return x * lax.rsqrt(jnp.mean(x * x, axis=-1, keepdims=True) + RMS_EPS) * g


def _silu(x):
    return x * (1.0 / (1.0 + jnp.exp(-x)))


def _iota(shape, dim):
    return lax.broadcasted_iota(jnp.int32, shape, dim)


def _row_tile(m, want):
    t = min(m, want)
    assert m % t == 0, (m, t)
    return t


def _norm_mm_kernel(x_ref, g_ref, w_ref, o_ref):
    h = _rms(x_ref[...], g_ref[...])
    o_ref[...] = _bdot(h, w_ref[...])


def norm_mm(x, g, w, tm=512):
    m, d = x.shape
    n = w.shape[1]
    tm = _row_tile(m, tm)
    return pl.pallas_call(
        _norm_mm_kernel,
        grid=(m // tm,),
        in_specs=[pl.BlockSpec((tm, d), lambda i: (i, 0)),
                  pl.BlockSpec((1, d), lambda i: (0, 0)),
                  pl.BlockSpec((d, n), lambda i: (0, 0))],
        out_specs=pl.BlockSpec((tm, n), lambda i: (i, 0)),
        out_shape=jax.ShapeDtypeStruct((m, n), F32),
        compiler_params=_cparams("parallel"),
        name="norm_mm",
    )(x, g.reshape(1, d), w)


def _mm_res_kernel(a_ref, w_ref, r_ref, o_ref):
    o_ref[...] = r_ref[...] + _bdot(a_ref[...], w_ref[...])


def mm_res(a, w, r, tm=512):
    m, k = a.shape
    n = w.shape[1]
    tm = _row_tile(m, tm)
    return pl.pallas_call(
        _mm_res_kernel,
        grid=(m // tm,),
        in_specs=[pl.BlockSpec((tm, k), lambda i: (i, 0)),
                  pl.BlockSpec((k, n), lambda i: (0, 0)),
                  pl.BlockSpec((tm, n), lambda i: (i, 0))],
        out_specs=pl.BlockSpec((tm, n), lambda i: (i, 0)),
        out_shape=jax.ShapeDtypeStruct((m, n), F32),
        compiler_params=_cparams("parallel"),
        name="mm_res",
    )(a, w, r)


def _mlp_kernel(x_ref, g_ref, w1_ref, w2_ref, gf_ref, o_ref, *, fc, final_norm):
    x = x_ref[...]
    h = _rms(x, g_ref[...]).astype(BF16)
    acc = x
    for c in range(w1_ref.shape[1] // fc):
        a = jnp.dot(h, w1_ref[:, c * fc:(c + 1) * fc], preferred_element_type=F32)
        a = jnp.square(jnp.maximum(a, 0.0))
        acc = acc + _bdot(a, w2_ref[c * fc:(c + 1) * fc, :])
    if final_norm:
        acc = _rms(acc, gf_ref[...])
    o_ref[...] = acc


def mlp_res(x, g, w1, w2, gf, final_norm, tm=512, fc=1024):
    m, d = x.shape
    f = w1.shape[1]
    tm = _row_tile(m, tm)
    return pl.pallas_call(
        functools.partial(_mlp_kernel, fc=fc, final_norm=final_norm),
        grid=(m // tm,),
        in_specs=[pl.BlockSpec((tm, d), lambda i: (i, 0)),
                  pl.BlockSpec((1, d), lambda i: (0, 0)),
                  pl.BlockSpec((d, f), lambda i: (0, 0)),
                  pl.BlockSpec((f, d), lambda i: (0, 0)),
                  pl.BlockSpec((1, d), lambda i: (0, 0))],
        out_specs=pl.BlockSpec((tm, d), lambda i: (i, 0)),
        out_shape=jax.ShapeDtypeStruct((m, d), F32),
        compiler_params=_cparams("parallel"),
        name="mlp_res",
    )(x, g.reshape(1, d), w1, w2, gf.reshape(1, d))


def _rope_tables(pos):
    half = HD_A // 2
    inv = ROPE_THETA ** (-jnp.arange(half, dtype=F32) / half)
    ang = pos.astype(F32)[:, None] * inv[None, :]
    cos = jnp.tile(jnp.cos(ang), (1, LANES // half))
    sin = jnp.sin(ang)
    sin = jnp.tile(jnp.concatenate([-sin, sin], axis=1), (1, LANES // HD_A))
    return cos, sin


def _rope_slab(x, cos, sin):
    half = HD_A // 2
    first = (_iota(x.shape, 1) % HD_A) < half
    swapped = jnp.where(first, pltpu.roll(x, LANES - half, 1), pltpu.roll(x, half, 1))
    return x * cos + swapped * sin


def _attn_prompt_kernel(q_ref, k_ref, v_ref, cos_ref, sin_ref, o_ref, lse_ref, ko_ref, vo_ref,
                        qr, kr, *, s, dil, keep, units):
    nb = s // (dil * A_BLK)
    span = A_BLK * dil
    cos, sin = cos_ref[...], sin_ref[...]
    k_rot = _rope_slab(k_ref[...], cos, sin)
    kr[...] = k_rot
    qr[...] = _rope_slab(q_ref[...], cos, sin) * (HD_A ** -0.5)
    ko_ref[...] = k_rot[s - keep:, :]
    vo_ref[...] = v_ref[s - keep:, :]
    head0 = _iota((A_BLK, LANES), 1) < HD_A
    qi = _iota((A_BLK, 2 * A_BLK), 0)
    kj = _iota((A_BLK, 2 * A_BLK), 1)
    dist = A_BLK + qi - kj
    band = (dist >= 0) & (dist <= A_BLK)
    cur = kj >= A_BLK

    def rows(start):
        if dil == 1:
            return pl.ds(pl.multiple_of(start, A_BLK), A_BLK)
        return pl.ds(start, A_BLK, stride=dil)

    def step(it, carry):
        starts, qs, kcs, vcs, masks = [], [], [], [], []
        for j in range(units):
            u = it * units + j
            r, i = u // nb, u % nb
            start = r + i * span
            prev = jnp.where(i > 0, start - span, start)
            starts.append(start)
            qs.append(qr[rows(start), :].astype(BF16))
            kcs.append(jnp.concatenate([kr[rows(prev), :], kr[rows(start), :]], axis=0).astype(BF16))
            vcs.append(jnp.concatenate([v_ref[rows(prev), :], v_ref[rows(start), :]],
                                       axis=0).astype(BF16))
            masks.append(band & (cur | (i > 0)))
        chains = [(j, hsel) for j in range(units) for hsel in (head0, ~head0)]
        sc = [jnp.where(masks[j], _bdot_nt(jnp.where(hsel, qs[j], jnp.zeros_like(qs[j])), kcs[j]),
                        NEG_INF) for j, hsel in chains]
        mx = [jnp.max(x, axis=-1, keepdims=True) for x in sc]
        ex = [jnp.exp(x - m) for x, m in zip(sc, mx)]
        den = [jnp.sum(e, axis=-1, keepdims=True) for e in ex]
        pv = [_bdot(e, vcs[j]) for e, (j, _) in zip(ex, chains)]
        for j in range(units):
            a, b = 2 * j, 2 * j + 1
            o_ref[rows(starts[j]), :] = jnp.where(head0, pv[a] / den[a], pv[b] / den[b])
            lse_ref[rows(starts[j]), :] = jnp.where(
                head0, jnp.broadcast_to(mx[a] + jnp.log(den[a]), (A_BLK, LANES)),
                jnp.broadcast_to(mx[b] + jnp.log(den[b]), (A_BLK, LANES)))
        return carry

    lax.fori_loop(0, dil * nb // units, step, 0)


def attn_prompt(qkv, g, window, dil, cos, sin, units=4):
    b, s, width = qkv.shape
    assert window == A_BLK * dil and s % window == 0 and (s // A_BLK) % units == 0
    nslab = A_GW // LANES
    assert width == 9 * A_GW

    def col_spec(c):
        return pl.BlockSpec((None, s, LANES), lambda i, p: (i, 0, (g * 3 + c) * nslab + p))

    tab_spec = pl.BlockSpec((s, LANES), lambda i, p: (0, 0))
    out_spec = pl.BlockSpec((None, s, LANES), lambda i, p: (i, 0, p))
    keep_spec = pl.BlockSpec((None, window, LANES), lambda i, p: (i, 0, p))
    return pl.pallas_call(
        functools.partial(_attn_prompt_kernel, s=s, dil=dil, keep=window, units=units),
        grid=(b, nslab),
        in_specs=[col_spec(0), col_spec(1), col_spec(2), tab_spec, tab_spec],
        out_specs=[out_spec, out_spec, keep_spec, keep_spec],
        out_shape=[jax.ShapeDtypeStruct((b, s, A_GW), F32),
                   jax.ShapeDtypeStruct((b, s, A_GW), F32),
                   jax.ShapeDtypeStruct((b, window, A_GW), F32),
                   jax.ShapeDtypeStruct((b, window, A_GW), F32)],
        scratch_shapes=[pltpu.VMEM((s, LANES), F32), pltpu.VMEM((s, LANES), F32)],
        compiler_params=_cparams("parallel", "parallel"),
        name=f"attn_prompt_g{g}",
    )(qkv, qkv, qkv, cos, sin)


def _norm_mm_t_kernel(wt_ref, x_ref, g_ref, o_ref):
    h = _rms(x_ref[...], g_ref[...])
    o_ref[...] = _bdot_nt(wt_ref[...], h)


def norm_mm_t(wt, x, g, tn=768):
    n, d = wt.shape
    m = x.shape[0]
    tn = _row_tile(n, tn)
    return pl.pallas_call(
        _norm_mm_t_kernel,
        grid=(n // tn,),
        in_specs=[pl.BlockSpec((tn, d), lambda i: (i, 0)),
                  pl.BlockSpec((m, d), lambda i: (0, 0)),
                  pl.BlockSpec((1, d), lambda i: (0, 0))],
        out_specs=pl.BlockSpec((tn, m), lambda i: (i, 0)),
        out_shape=jax.ShapeDtypeStruct((n, m), F32),
        compiler_params=_cparams("parallel"),
        name="norm_mm_t",
    )(wt, x, g.reshape(1, d))


def _attn_sample_kernel(*refs, t_new, length, window, dil, aliased):
    if aliased:
        refs = refs[1:]
    (q_ref, kt_ref, vt_ref, cos_ref, sin_ref, cost_ref, sint_ref, cache_ref,
     o_ref, lse_ref, out_ref) = refs
    b = pl.program_id(0)
    per_blk = LANES // t_new
    new0 = LANES - t_new
    shift = new0 - (b % per_blk) * t_new
    half = HD_A // 2
    ncol = length // LANES
    scale = HD_A ** -0.5
    cos, sin = cos_ref[...], sin_ref[...]
    cos_t, sin_t = cost_ref[...], sint_ref[...]
    q = jnp.concatenate(
        [_rope_slab(q_ref[0, :, p * LANES:(p + 1) * LANES], cos, sin) for p in range(A_GW // LANES)],
        axis=1) * scale
    q8 = jnp.concatenate([q, jnp.zeros((8 - t_new, A_GW), F32)], axis=0)
    lane = _iota((HD_A, LANES), 1)
    is_new = lane >= new0
    t_b = _iota((8, length), 0)
    l_b = _iota((8, length), 1)
    valid_b = ((l_b & (dil - 1)) == (t_b & (dil - 1))) & (l_b >= t_b + (length - window))
    t_n = _iota((8, LANES), 0)
    u_n = _iota((8, LANES), 1) - new0
    valid_n = (u_n >= 0) & (u_n <= t_n) & (((t_n - u_n) & (dil - 1)) == 0)
    heads = range(HG_A)
    k_new, v_new = [], []
    for h in heads:
        hr = slice(h * HD_A, (h + 1) * HD_A)
        kt = kt_ref[hr, :]
        x1, x2 = kt[:half], kt[half:]
        kt = jnp.concatenate([x1 * cos_t - x2 * sin_t, x2 * cos_t + x1 * sin_t], axis=0)
        k_new.append(jnp.where(is_new, pltpu.roll(kt, shift, 1), 0.0))
        v_new.append(jnp.where(is_new, pltpu.roll(vt_ref[hr, :], shift, 1), 0.0))
    q_h = [q8[:, h * HD_A:(h + 1) * HD_A] for h in heads]
    s_b = [jnp.where(valid_b, _bdot(q_h[h], cache_ref[0, 0, 0, h]), NEG_INF) for h in heads]
    s_n = [jnp.where(valid_n, _bdot(q_h[h], k_new[h]), NEG_INF) for h in heads]
    m = [jnp.maximum(jnp.max(s_b[h], axis=-1, keepdims=True), jnp.max(s_n[h], axis=-1, keepdims=True))
         for h in heads]
    e_b = [jnp.exp(s_b[h] - m[h]) for h in heads]
    e_n = [jnp.exp(s_n[h] - m[h]) for h in heads]
    l = [jnp.sum(e_b[h], axis=-1, keepdims=True) + jnp.sum(e_n[h], axis=-1, keepdims=True) for h in heads]
    o = [(_bdot_nt(e_b[h], cache_ref[0, 0, 1, h]) + _bdot_nt(e_n[h], v_new[h])) / l[h] for h in heads]
    o_ref[0] = jnp.concatenate(o, axis=1)[:t_new]
    lse_ref[0] = jnp.concatenate(
        [jnp.broadcast_to(m[h] + jnp.log(l[h]), (8, HD_A)) for h in heads], axis=1)[:t_new]
    for h in heads:
        for c, new in enumerate((k_new[h], v_new[h])):
            rolled = [pltpu.roll(cache_ref[0, 0, c, h, :, i * LANES:(i + 1) * LANES], new0, 1)
                      for i in range(ncol)]
            for i in range(ncol):
                nxt = rolled[i + 1] if i + 1 < ncol else new
                out_ref[0, 0, c, h, :, i * LANES:(i + 1) * LANES] = jnp.where(is_new, nxt, rolled[i])


def attn_sample(g, layer, qkv, kvt_new, tabs, tabs_t, cache_t, prev_out):
    window, dil = A_GROUPS[g]
    bd_, t_new, width = qkv.shape
    n_layers, _, _, _, _, length = cache_t.shape
    assert LANES % t_new == 0 and length % LANES == 0 and t_new <= 8 and dil & (dil - 1) == 0
    per_blk = LANES // t_new
    slabs = width // A_GW
    kv_rows = kvt_new.shape[0] // A_GW
    aliased = prev_out is not None
    in_specs = [pl.BlockSpec((1, t_new, A_GW), lambda i: (i, 0, g * 3)),
                pl.BlockSpec((A_GW, LANES), lambda i: (2 * g, i // per_blk)),
                pl.BlockSpec((A_GW, LANES), lambda i: (2 * g + 1, i // per_blk)),
                pl.BlockSpec((t_new, LANES), lambda i: (0, 0)),
                pl.BlockSpec((t_new, LANES), lambda i: (0, 0)),
                pl.BlockSpec((HD_A // 2, LANES), lambda i: (0, 0)),
                pl.BlockSpec((HD_A // 2, LANES), lambda i: (0, 0)),
                pl.BlockSpec((1, 1, 2, HG_A, HD_A, length), lambda i: (layer, i, 0, 0, 0, 0))]
    args = [qkv, kvt_new, kvt_new, *tabs, *tabs_t, cache_t]
    if aliased:
        in_specs.insert(0, pl.BlockSpec(memory_space=pl.ANY))
        args.insert(0, prev_out)
    assert slabs == 3 * len(A_GROUPS) and kv_rows == 2 * len(A_GROUPS)
    return pl.pallas_call(
        functools.partial(_attn_sample_kernel, t_new=t_new, length=length, window=window, dil=dil,
                          aliased=aliased),
        grid=(bd_,),
        in_specs=in_specs,
        out_specs=[pl.BlockSpec((1, t_new, A_GW), lambda i: (i, 0, 0)),
                   pl.BlockSpec((1, t_new, A_GW), lambda i: (i, 0, 0)),
                   pl.BlockSpec((1, 1, 2, HG_A, HD_A, length), lambda i: (layer, i, 0, 0, 0, 0))],
        out_shape=[jax.ShapeDtypeStruct((bd_, t_new, A_GW), F32),
                   jax.ShapeDtypeStruct((bd_, t_new, A_GW), F32),
                   jax.ShapeDtypeStruct(cache_t.shape, F32)],
        input_output_aliases={0: 2} if aliased else {},
        compiler_params=_cparams("arbitrary"),
        name=f"attn_sample_g{g}",
    )(*args)


def _combine_out_kernel(*refs, ng):
    o_refs, l_refs = refs[:ng], refs[ng:2 * ng]
    w_ref, r_ref, out_ref = refs[2 * ng:]
    ls = [l[...] for l in l_refs]
    m = functools.reduce(jnp.maximum, ls)
    es = [jnp.exp(l - m) for l in ls]
    inv = 1.0 / functools.reduce(jnp.add, es)
    a = jnp.concatenate([o[...] * (e * inv) for o, e in zip(o_refs, es)], axis=1)
    out_ref[...] = r_ref[...] + _bdot(a, w_ref[...])


def combine_out(os_, lses, w, r, tm=512):
    ng = len(os_)
    m = os_[0].shape[0]
    gw = A_GW
    d = w.shape[1]
    tm = _row_tile(m, tm)
    slabs = [pl.BlockSpec((tm, gw), lambda i, c=(g if o.shape[1] > gw else 0): (i, c))
             for g, o in enumerate(os_)]
    return pl.pallas_call(
        functools.partial(_combine_out_kernel, ng=ng),
        grid=(m // tm,),
        in_specs=slabs * 2 + [pl.BlockSpec((ng * gw, d), lambda i: (0, 0)),
                              pl.BlockSpec((tm, d), lambda i: (i, 0))],
        out_specs=pl.BlockSpec((tm, d), lambda i: (i, 0)),
        out_shape=jax.ShapeDtypeStruct((m, d), F32),
        compiler_params=_cparams("parallel"),
        name="combine_out",
    )(*os_, *lses, w, r)


def _split3_dot(a01, b):
    a = a01.astype(BF16)
    b1 = b.astype(BF16)
    r1 = b - b1.astype(F32)
    b2 = r1.astype(BF16)
    b3 = (r1 - b2.astype(F32)).astype(BF16)
    dot = functools.partial(jnp.dot, preferred_element_type=F32)
    return dot(a, b1) + dot(a, b2) + dot(a, b3)


def _delta_kernel(q_ref, k_ref, v_ref, z_ref, gt_ref, hq_ref, hk_ref, hv_ref, cw_q, cw_k, cw_v,
                  alog_ref, dtb_ref, nw_ref, s0_ref, o_ref, s_ref, xq, xk, xv,
                  *, t_len, t_valid, chunk, cg, hb, n_heads):
    halo = 8
    for src, hal, dst in ((q_ref, hq_ref, xq), (k_ref, hk_ref, xk), (v_ref, hv_ref, xv)):
        dst[0:halo, :] = hal[0]
        dst[halo:, :] = src[0]
    rows = cg * chunk
    ci = _iota((chunk, chunk), 0)
    si = _iota((chunk, chunk), 1)
    ltri = (ci >= si).astype(F32)
    strict = ci > si
    eye = (ci == si).astype(F32)
    merge_masks = []
    for lg in range(chunk.bit_length() - 1):
        merge_masks.append(((ci >> (lg + 1)) == (si >> (lg + 1)))
                           & (((ci >> lg) & 1) == 1) & (((si >> lg) & 1) == 0))
    lane_r = _iota((rows, LANES), 1)
    lane_c = _iota((chunk, LANES), 1)
    nw = nw_ref[...]
    head_base = pl.program_id(1) * hb

    def conv_act(xs, cw, r0, cols):
        win = xs[pl.ds(r0, rows + halo), cols]
        w = cw[:, cols]
        c = win[halo - (CONV_W - 1):halo - (CONV_W - 1) + rows, :] * w[0:1, :]
        for j in range(1, CONV_W):
            off = halo - (CONV_W - 1) + j
            c = c + win[off:off + rows, :] * w[j:j + 1, :]
        return _silu(c)

    def l2n(x):
        return x * lax.rsqrt(jnp.sum(x * x, axis=-1, keepdims=True) + L2_EPS)

    def lane_col(x, lane_ids, idx):
        return jnp.sum(jnp.where(lane_ids == idx, x, 0.0), axis=-1, keepdims=True)

    def group(i, states):
        r0 = pl.multiple_of(i * rows, rows)
        gate = gt_ref[0, pl.ds(r0, rows), :]
        x = gate + dtb_ref[...]
        softplus = jnp.maximum(x, 0.0) + jnp.log(1.0 + jnp.exp(-jnp.abs(x)))
        g_all = -jnp.exp(alog_ref[...]) * softplus
        beta_all = 1.0 / (1.0 + jnp.exp(-gate))
        if t_valid < t_len:
            valid = (r0 + _iota((rows, 1), 0)) < t_valid
            g_all = jnp.where(valid, g_all, 0.0)
            beta_all = jnp.where(valid, beta_all, 0.0)
        chunks = [slice(c * chunk, (c + 1) * chunk) for c in range(cg)]
        g_cum_all = [_split3_dot(ltri, g_all[rs]) for rs in chunks]
        heads = []
        for hh in range(hb):
            cols = slice(hh * DK_B, (hh + 1) * DK_B)
            k_all = l2n(conv_act(xk, cw_k, r0, cols))
            if t_valid < t_len:
                k_all = jnp.where(valid, k_all, 0.0)
            heads.append((l2n(conv_act(xq, cw_q, r0, cols)) * (DK_B ** -0.5), k_all,
                          conv_act(xv, cw_v, r0, cols),
                          lane_col(beta_all, lane_r, n_heads + head_base + hh)))
        units = [(hh, c) for c in range(cg) for hh in range(hb)]
        q_, k_, v_, kb_, gc_, dec_ = [], [], [], [], [], []
        for hh, c in units:
            q_all, k_all, v_all, beta_col = heads[hh]
            rs = chunks[c]
            g_c = jnp.broadcast_to(lane_col(g_cum_all[c], lane_c, head_base + hh), (chunk, LANES))
            g_r = jnp.transpose(jnp.concatenate(
                [g_c, jnp.zeros((LANES - chunk, LANES), F32)], axis=0))[:chunk, :chunk]
            diff = jnp.where(ci >= si, g_c[:, :chunk] - g_r, 0.0)
            q_.append(q_all[rs]); k_.append(k_all[rs]); v_.append(v_all[rs] * beta_col[rs])
            kb_.append(k_all[rs] * beta_col[rs]); gc_.append(g_c)
            dec_.append(jnp.where(ci >= si, jnp.exp(diff), 0.0))
        n_u = len(units)
        a_ = [jnp.where(strict, _bdot_nt(kb_[u], k_[u]) * dec_[u], 0.0) for u in range(n_u)]
        t_ = [eye - jnp.where(merge_masks[0], a_[u], 0.0) for u in range(n_u)]
        for mask in merge_masks[1:]:
            low = [_bdot(jnp.where(mask, a_[u], 0.0), t_[u]) for u in range(n_u)]
            t_ = [t_[u] - _bdot(t_[u], low[u]) for u in range(n_u)]
        eg_ = [jnp.exp(gc_[u]) for u in range(n_u)]
        uw_ = [_bdot(t_[u], jnp.concatenate([v_[u], kb_[u] * eg_[u]], axis=1)) for u in range(n_u)]
        attn_ = [_bdot_nt(q_[u], k_[u]) * dec_[u] for u in range(n_u)]
        states = list(states)
        for c in range(cg):
            us = [u for u in range(n_u) if units[u][1] == c]
            glast = {u: gc_[u][chunk - 1:chunk, :] for u in us}
            ws = {u: _bdot(uw_[u][:, DK_B:], states[units[u][0]]) for u in us}
            qs = {u: _bdot(q_[u] * eg_[u], states[units[u][0]]) for u in us}
            vn = {u: uw_[u][:, :DK_B] - ws[u] for u in us}
            for u in us:
                hh = units[u][0]
                kd = k_[u] * jnp.exp(glast[u] - gc_[u])
                o = qs[u] + _bdot(attn_[u], vn[u])
                states[hh] = states[hh] * jnp.exp(glast[u]) + _bdot_tn(kd, vn[u])
                o = o * lax.rsqrt(jnp.mean(o * o, axis=-1, keepdims=True) + RMS_EPS) * nw
                cols = slice(hh * DK_B, (hh + 1) * DK_B)
                z = z_ref[0, pl.ds(r0 + c * chunk, chunk), cols]
                o_ref[0, pl.ds(r0 + c * chunk, chunk), cols] = o * _silu(z)
        return tuple(states)

    init = tuple(s0_ref[0, hh] for hh in range(hb))
    final = lax.fori_loop(0, t_len // rows, group, init)
    for hh in range(hb):
        s_ref[0, hh] = final[hh]


def delta_core(proj, conv_halo, s0, conv_w, a_log, dt_bias, norm_w, *, t_valid, chunk, cg, hb):
    b, t_len, width = proj.shape
    n_heads = s0.shape[1]
    hw = n_heads * DK_B
    assert width == 4 * hw + LANES and t_len % (cg * chunk) == 0 and n_heads % hb == 0
    nhb = n_heads // hb
    bw = hb * DK_B

    def col_spec(rows, c):
        return pl.BlockSpec((1, rows, bw), lambda i, h: (i, 0, c * nhb + h))

    def row_vec(x):
        return jnp.pad(x.astype(F32), (0, LANES - x.shape[0])).reshape(1, LANES)

    vec_spec = pl.BlockSpec((1, LANES), lambda i, h: (0, 0))
    cw_specs = [pl.BlockSpec((CONV_W, bw), lambda i, h, c=c: (0, c * nhb + h)) for c in range(3)]
    return pl.pallas_call(
        functools.partial(_delta_kernel, t_len=t_len, t_valid=t_valid, chunk=chunk, cg=cg,
                          hb=hb, n_heads=n_heads),
        grid=(b, nhb),
        in_specs=[col_spec(t_len, 0), col_spec(t_len, 1), col_spec(t_len, 2), col_spec(t_len, 3),
                  pl.BlockSpec((1, t_len, LANES), lambda i, h: (i, 0, 4 * hw // LANES)),
                  col_spec(8, 0), col_spec(8, 1), col_spec(8, 2)] + cw_specs
                 + [vec_spec, vec_spec, vec_spec,
                    pl.BlockSpec((1, hb, DK_B, DK_B), lambda i, h: (i, h, 0, 0))],
        out_specs=[pl.BlockSpec((1, t_len, bw), lambda i, h: (i, 0, h)),
                   pl.BlockSpec((1, hb, DK_B, DK_B), lambda i, h: (i, h, 0, 0))],
        out_shape=[jax.ShapeDtypeStruct((b, t_len, hw), F32),
                   jax.ShapeDtypeStruct(s0.shape, F32)],
        scratch_shapes=[pltpu.VMEM((t_len + 8, bw), F32)] * 3,
        compiler_params=_cparams("parallel", "parallel"),
        name="delta_core",
    )(proj, proj, proj, proj, proj, conv_halo, conv_halo, conv_halo, conv_w, conv_w, conv_w,
      row_vec(a_log), row_vec(dt_bias), norm_w.astype(F32).reshape(1, DK_B), s0)


def _pool_prompt_kernel(x_ref, xh_ref, g_ref, w_ref, sc_ref, o_ref, tail_ref, hs, *, tm):
    i = pl.program_id(1)
    g = g_ref[...]
    x = x_ref[0]
    h = _rms(x, g)
    hs[0:16, :] = jnp.where(i > 0, _rms(xh_ref[0], g), 0.0)
    hs[16:, :] = h
    tail_ref[0] = h[tm - 16:, :]
    pos = (i * tm + _iota((tm, 1), 0) + 1).astype(F32)
    cg = x.shape[1] // len(POOL_WINDOWS)
    for gi, w in enumerate(POOL_WINDOWS):
        cols = slice(gi * cg, (gi + 1) * cg)
        tot = h[:, cols]
        for j in range(1, w):
            tot = tot + hs[16 - j:16 - j + tm, cols]
        d = tot / jnp.minimum(float(w), pos) - h[:, cols]
        o_ref[0, :, cols] = x[:, cols] + _bdot(d, w_ref[gi]) * sc_ref[:, cols]


def pool_prompt(x, g, w_grp, scale, tm=512):
    b, s, d = x.shape
    tm = _row_tile(s, tm)
    per = tm // 16
    ng, cgw, _ = w_grp.shape
    return pl.pallas_call(
        functools.partial(_pool_prompt_kernel, tm=tm),
        grid=(b, s // tm),
        in_specs=[pl.BlockSpec((1, tm, d), lambda i, j: (i, j, 0)),
                  pl.BlockSpec((1, 16, d), lambda i, j: (i, jnp.maximum(j * per - 1, 0), 0)),
                  pl.BlockSpec((1, d), lambda i, j: (0, 0)),
                  pl.BlockSpec((ng, cgw, cgw), lambda i, j: (0, 0, 0)),
                  pl.BlockSpec((1, d), lambda i, j: (0, 0))],
        out_specs=[pl.BlockSpec((1, tm, d), lambda i, j: (i, j, 0)),
                   pl.BlockSpec((1, 16, d), lambda i, j: (i, 0, 0))],
        out_shape=[jax.ShapeDtypeStruct((b, s, d), F32),
                   jax.ShapeDtypeStruct((b, 16, d), F32)],
        scratch_shapes=[pltpu.VMEM((tm + 16, d), F32)],
        compiler_params=_cparams("parallel", "arbitrary"),
        name="pool_prompt",
    )(x, x, g.reshape(1, d), w_grp, scale.reshape(1, d))


def _pool_sample_kernel(x_ref, pre_ref, g_ref, w_ref, sc_ref, o_ref, h_ref, *, t_new, n_pre):
    g = g_ref[...]
    xs = [x_ref[t] for t in range(t_new)]
    rows = [pre_ref[r] for r in range(n_pre)] + [_rms(x, g) for x in xs]
    cg = xs[0].shape[1] // len(POOL_WINDOWS)
    for t in range(t_new):
        h_ref[t] = rows[n_pre + t]
        for gi, w in enumerate(POOL_WINDOWS):
            cols = slice(gi * cg, (gi + 1) * cg)
            lo = max(n_pre + t - w + 1, 0)
            tot = rows[lo][:, cols]
            for r in range(lo + 1, n_pre + t + 1):
                tot = tot + rows[r][:, cols]
            d = tot / float(min(w, t + 1 + n_pre)) - rows[n_pre + t][:, cols]
            o_ref[t, :, cols] = xs[t][:, cols] + _bdot(d, w_ref[gi]) * sc_ref[:, cols]


def pool_sample(x_t, pre_t, g, w_grp, scale):
    t_new, bd_, d = x_t.shape
    n_pre = pre_t.shape[0]
    ng, cgw, _ = w_grp.shape
    full = lambda shape: pl.BlockSpec(shape, lambda i: (0,) * len(shape))
    return pl.pallas_call(
        functools.partial(_pool_sample_kernel, t_new=t_new, n_pre=n_pre),
        grid=(1,),
        in_specs=[full((t_new, bd_, d)), full((n_pre, bd_, d)), full((1, d)),
                  full((ng, cgw, cgw)), full((1, d))],
        out_specs=[full((t_new, bd_, d)), full((t_new, bd_, d))],
        out_shape=[jax.ShapeDtypeStruct((t_new, bd_, d), F32),
                   jax.ShapeDtypeStruct((t_new, bd_, d), F32)],
        compiler_params=_cparams("arbitrary"),
        name="pool_sample",
    )(x_t, pre_t, g.reshape(1, d), w_grp, scale.reshape(1, d))


def _rope_tables_t(pos, n_cols):
    half = HD_A // 2
    inv = ROPE_THETA ** (-jnp.arange(half, dtype=F32) / half)
    ang = inv[:, None] * jnp.tile(pos.astype(F32), n_cols // pos.shape[0])[None, :]
    return jnp.cos(ang), jnp.sin(ang)


def _mixer_a(xp, xs, g1, w_in, wt_kv, w_out, layer, caches_t, prev_outs, tabs_p, tabs_s, tabs_st):
    bp, sp, d = xp.shape
    bs, ts, _ = xs.shape
    xp2, xs2 = xp.reshape(-1, d), xs.reshape(-1, d)
    qkv_p = norm_mm(xp2, g1, w_in).reshape(bp, sp, -1)
    qkv_s = norm_mm(xs2, g1, w_in).reshape(bs, ts, -1)
    kvt_new = norm_mm_t(wt_kv, xs2, g1)
    os_, lses, kv_p = [], [], []
    os_s, lses_s, kv_s = [], [], []
    for g, (window, dil) in enumerate(A_GROUPS):
        assert sp >= window and caches_t[g].shape[-1] >= window
        o, lse, k_keep, v_keep = attn_prompt(qkv_p, g, window, dil, *tabs_p)
        os_.append(o.reshape(-1, A_GW))
        lses.append(lse.reshape(-1, A_GW))
        kv_p.append(jnp.stack([k_keep.reshape(bp, window, HG_A, HD_A),
                               v_keep.reshape(bp, window, HG_A, HD_A)], axis=2))
        o, lse, out = attn_sample(g, layer, qkv_s, kvt_new, tabs_s, tabs_st, caches_t[g],
                                  None if prev_outs is None else prev_outs[g])
        os_s.append(o.reshape(-1, A_GW))
        lses_s.append(lse.reshape(-1, A_GW))
        kv_s.append(out)
    yp = combine_out(os_, lses, w_out, xp2).reshape(xp.shape)
    ys = combine_out(os_s, lses_s, w_out, xs2).reshape(xs.shape)
    return yp, ys, kv_p, kv_s


def _mixer_b(xp, xs, g1, w_in, conv_w, a_log, dt_bias, norm_w, w_out, conv_s0, state_s0):
    bp, sp, d = xp.shape
    bs, ts, _ = xs.shape
    n_heads = state_s0.shape[1]
    qkv_w = 3 * n_heads * DK_B
    xp2, xs2 = xp.reshape(-1, d), xs.reshape(-1, d)
    proj_p = norm_mm(xp2, g1, w_in).reshape(bp, sp, -1)
    proj_s = norm_mm(xs2, g1, w_in).reshape(bs, ts, -1)
    consts = (conv_w, a_log, dt_bias, norm_w)
    o_p, st_p = delta_core(proj_p, jnp.zeros((bp, 8, qkv_w), F32),
                           jnp.zeros((bp,) + state_s0.shape[1:], F32), *consts,
                           t_valid=sp, chunk=CHUNK_B, cg=8, hb=2)
    yp = mm_res(o_p.reshape(bp * sp, -1), w_out, xp2).reshape(xp.shape)
    t_pad = 8
    assert CONV_W - 1 <= ts <= t_pad
    o_s, st_s = delta_core(jnp.pad(proj_s, ((0, 0), (0, t_pad - ts), (0, 0))),
                           jnp.pad(conv_s0, ((0, 0), (8 - (CONV_W - 1), 0), (0, 0))),
                           state_s0, *consts, t_valid=ts, chunk=t_pad, cg=1, hb=n_heads)
    ys = mm_res(o_s[:, :ts].reshape(bs * ts, -1), w_out, xs2).reshape(xs.shape)
    conv_p = proj_p[:, sp - (CONV_W - 1):, :qkv_w]
    conv_s = proj_s[:, ts - (CONV_W - 1):, :qkv_w]
    return yp, ys, (st_p, conv_p), (st_s, conv_s)


def _mixer_c(xp, xs, g1, w_grp, scale, pool_s0):
    ts = xs.shape[1]
    yp, tail = pool_prompt(xp, g1, w_grp, scale)
    pool_p = tail[:, tail.shape[1] - POOL_BUF:]
    y_t, h_t = pool_sample(xs.transpose(1, 0, 2), pool_s0.transpose(1, 0, 2), g1, w_grp, scale)
    pool_s = jnp.concatenate([pool_s0, h_t.transpose(1, 0, 2)], axis=1)[:, ts:]
    return yp, y_t.transpose(1, 0, 2), pool_p, pool_s


def kernel(x_prompt, x_sample, cache_a_kv1, cache_a_kv2, cache_a_kv3, state_b_s, state_b_conv, state_c_pool, norm1, norm2, norm_f, a_w_in, a_w_out, b_w_in, b_conv, b_a_log, b_dt_bias, b_norm, b_w_out, c_w, c_scale, mlp_w1, mlp_w2):
    xp, xs = x_prompt, x_sample
    depth = norm1.shape[0]
    d = xp.shape[-1]
    past_len = cache_a_kv3.shape[2]
    tabs_p = _rope_tables(jnp.arange(xp.shape[1]))
    pos_s = past_len + jnp.arange(xs.shape[1])
    tabs_s = _rope_tables(pos_s)
    tabs_st = _rope_tables_t(pos_s, LANES)
    caches_t = [jnp.transpose(c, (0, 1, 3, 4, 5, 2)) for c in (cache_a_kv1, cache_a_kv2, cache_a_kv3)]
    kv_cols = np.concatenate([np.arange(g * 3 * A_GW + A_GW, (g + 1) * 3 * A_GW)
                              for g in range(len(A_GROUPS))])
    b_pad = (-b_w_in.shape[2]) % LANES
    a_p, a_s, b_p, b_s, c_p, c_s = [], [], [], [], [], []
    kv_s = None
    for i in range(depth):
        kind, j = i % 3, i // 3
        if kind == 0:
            w_in = a_w_in[j].astype(BF16)
            xp, xs, kv_p, kv_s = _mixer_a(xp, xs, norm1[i], w_in, w_in[:, kv_cols].T,
                                          a_w_out[j].astype(BF16), j, caches_t, kv_s,
                                          tabs_p, tabs_s, tabs_st)
            a_p.append(kv_p)
        elif kind == 1:
            w_in = jnp.pad(b_w_in[j], ((0, 0), (0, b_pad))).astype(BF16)
            xp, xs, st_p, st_s = _mixer_b(xp, xs, norm1[i], w_in, b_conv[j], b_a_log[j], b_dt_bias[j],
                                          b_norm[j], b_w_out[j].astype(BF16), state_b_conv[j], state_b_s[j])
            b_p.append(st_p)
            b_s.append(st_s)
        else:
            xp, xs, pool_p, pool_s = _mixer_c(xp, xs, norm1[i], c_w[j].astype(BF16), c_scale[j],
                                              state_c_pool[j])
            c_p.append(pool_p)
            c_s.append(pool_s)
        w1, w2 = mlp_w1[i].astype(BF16), mlp_w2[i].astype(BF16)
        last = i == depth - 1
        xp = mlp_res(xp.reshape(-1, d), norm2[i], w1, w2, norm_f, last).reshape(xp.shape)
        xs = mlp_res(xs.reshape(-1, d), norm2[i], w1, w2, norm_f, last).reshape(xs.shape)
    stack = lambda items, k: jnp.stack([e[k] for e in items])
    return (xp, xs,
            stack(a_p, 0), stack(a_p, 1), stack(a_p, 2),
            stack(b_p, 0), stack(b_p, 1), jnp.stack(c_p),
            *[jnp.transpose(c, (0, 1, 5, 2, 3, 4)) for c in kv_s],
            stack(b_s, 0), stack(b_s, 1), jnp.stack(c_s))
```

```python
import functools
import math

import numpy as np
import jax
import jax.numpy as jnp
from jax import lax
from jax.experimental import pallas as pl
from jax.experimental.pallas import tpu as pltpu

F32 = jnp.float32
BF16 = jnp.bfloat16

RMS_EPS = 1e-6
L2_EPS = 1e-6
ROPE_THETA = 10000.0
A_GROUPS = ((128, 1), (512, 4), (2048, 16))
HG_A = 6
HD_A = 64
A_GW = HG_A * HD_A
A_BLK = 128
CONV_W = 4
DK_B = 128
CHUNK_B = 64
POOL_WINDOWS = (2, 4, 8, 16)
POOL_BUF = max(POOL_WINDOWS) - 1
LANES = 128
VMEM_LIMIT = 56 * 1024 * 1024
SAMPLE_SEQS_PER_STEP = 4
NEG_INF = float("-inf")


def _cparams(*sem):
    return pltpu.CompilerParams(dimension_semantics=sem, vmem_limit_bytes=VMEM_LIMIT)


def _bdot(a, b):
    return jnp.dot(a.astype(BF16), b.astype(BF16), preferred_element_type=F32)


def _bdot_nt(a, b):
    return lax.dot_general(a.astype(BF16), b.astype(BF16), (((1,), (1,)), ((), ())),
                           preferred_element_type=F32)


def _bdot_tn(a, b):
    return lax.dot_general(a.astype(BF16), b.astype(BF16), (((0,), (0,)), ((), ())),
                           preferred_element_type=F32)


def _rms(x, g):
    return x * lax.rsqrt(jnp.mean(x * x, axis=-1, keepdims=True) + RMS_EPS) * g


def _silu(x):
    return x * (1.0 / (1.0 + jnp.exp(-x)))


def _iota(shape, dim):
    return lax.broadcasted_iota(jnp.int32, shape, dim)


def _row_tile(m, want):
    t = min(m, want)
    assert m % t == 0, (m, t)
    return t


def _norm_mm_kernel(x_ref, g_ref, w_ref, o_ref):
    h = _rms(x_ref[...], g_ref[...])
    o_ref[...] = _bdot(h, w_ref[...])


def norm_mm(x, g, w, tm=512):
    m, d = x.shape
    n = w.shape[1]
    tm = _row_tile(m, tm)
    return pl.pallas_call(
        _norm_mm_kernel,
        grid=(m // tm,),
        in_specs=[pl.BlockSpec((tm, d), lambda i: (i, 0)),
                  pl.BlockSpec((1, d), lambda i: (0, 0)),
                  pl.BlockSpec((d, n), lambda i: (0, 0))],
        out_specs=pl.BlockSpec((tm, n), lambda i: (i, 0)),
        out_shape=jax.ShapeDtypeStruct((m, n), F32),
        compiler_params=_cparams("parallel"),
        name="norm_mm",
    )(x, g.reshape(1, d), w)


def _mm_res_kernel(a_ref, w_ref, r_ref, o_ref):
    o_ref[...] = r_ref[...] + _bdot(a_ref[...], w_ref[...])


def mm_res(a, w, r, tm=512):
    m, k = a.shape
    n = w.shape[1]
    tm = _row_tile(m, tm)
    return pl.pallas_call(
        _mm_res_kernel,
        grid=(m // tm,),
        in_specs=[pl.BlockSpec((tm, k), lambda i: (i, 0)),
                  pl.BlockSpec((k, n), lambda i: (0, 0)),
                  pl.BlockSpec((tm, n), lambda i: (i, 0))],
        out_specs=pl.BlockSpec((tm, n), lambda i: (i, 0)),
        out_shape=jax.ShapeDtypeStruct((m, n), F32),
        compiler_params=_cparams("parallel"),
        name="mm_res",
    )(a, w, r)


def _mlp_kernel(x_ref, g_ref, w1_ref, w2_ref, gf_ref, o_ref, *, fc, final_norm):
    x = x_ref[...]
    h = _rms(x, g_ref[...]).astype(BF16)
    acc = x
    for c in range(w1_ref.shape[1] // fc):
        a = jnp.dot(h, w1_ref[:, c * fc:(c + 1) * fc], preferred_element_type=F32)
        a = jnp.square(jnp.maximum(a, 0.0))
        acc = acc + _bdot(a, w2_ref[c * fc:(c + 1) * fc, :])
    if final_norm:
        acc = _rms(acc, gf_ref[...])
    o_ref[...] = acc


def mlp_res(x, g, w1, w2, gf, final_norm, tm=512, fc=1024):
    m, d = x.shape
    f = w1.shape[1]
    tm = _row_tile(m, tm)
    return pl.pallas_call(
        functools.partial(_mlp_kernel, fc=fc, final_norm=final_norm),
        grid=(m // tm,),
        in_specs=[pl.BlockSpec((tm, d), lambda i: (i, 0)),
                  pl.BlockSpec((1, d), lambda i: (0, 0)),
                  pl.BlockSpec((d, f), lambda i: (0, 0)),
                  pl.BlockSpec((f, d), lambda i: (0, 0)),
                  pl.BlockSpec((1, d), lambda i: (0, 0))],
        out_specs=pl.BlockSpec((tm, d), lambda i: (i, 0)),
        out_shape=jax.ShapeDtypeStruct((m, d), F32),
        compiler_params=_cparams("parallel"),
        name="mlp_res",
    )(x, g.reshape(1, d), w1, w2, gf.reshape(1, d))


def _rope_tables(pos):
    half = HD_A // 2
    inv = ROPE_THETA ** (-jnp.arange(half, dtype=F32) / half)
    ang = pos.astype(F32)[:, None] * inv[None, :]
    cos = jnp.tile(jnp.cos(ang), (1, LANES // half))
    sin = jnp.sin(ang)
    sin = jnp.tile(jnp.concatenate([-sin, sin], axis=1), (1, LANES // HD_A))
    return cos, sin


def _rope_slab(x, cos, sin):
    half = HD_A // 2
    first = (_iota(x.shape, 1) % HD_A) < half
    swapped = jnp.where(first, pltpu.roll(x, LANES - half, 1), pltpu.roll(x, half, 1))
    return x * cos + swapped * sin


def _attn_prompt_kernel(*refs, s, dil, keep, units, aliased):
    if aliased:
        refs = refs[1:]
    q_ref, k_ref, v_ref, cos_ref, sin_ref, o_ref, lse_ref, kv_ref, qr, kr = refs
    nb = s // (dil * A_BLK)
    span = A_BLK * dil
    cos, sin = cos_ref[...], sin_ref[...]
    k_rot = _rope_slab(k_ref[...], cos, sin)
    kr[...] = k_rot
    qr[...] = _rope_slab(q_ref[...], cos, sin) * (HD_A ** -0.5)
    kv_ref[0] = jnp.transpose(k_rot[s - keep:, :])
    kv_ref[1] = jnp.transpose(v_ref[s - keep:, :])
    head0 = _iota((A_BLK, LANES), 1) < HD_A
    qi = _iota((A_BLK, 2 * A_BLK), 0)
    kj = _iota((A_BLK, 2 * A_BLK), 1)
    dist = A_BLK + qi - kj
    band = (dist >= 0) & (dist <= A_BLK)
    cur = kj >= A_BLK

    def rows(start):
        if dil == 1:
            return pl.ds(pl.multiple_of(start, A_BLK), A_BLK)
        return pl.ds(start, A_BLK, stride=dil)

    def step(it, carry):
        starts, qs, kcs, vcs, masks = [], [], [], [], []
        for j in range(units):
            u = it * units + j
            r, i = u // nb, u % nb
            start = r + i * span
            prev = jnp.where(i > 0, start - span, start)
            starts.append(start)
            qs.append(qr[rows(start), :].astype(BF16))
            kcs.append(jnp.concatenate([kr[rows(prev), :], kr[rows(start), :]], axis=0).astype(BF16))
            vcs.append(jnp.concatenate([v_ref[rows(prev), :], v_ref[rows(start), :]],
                                       axis=0).astype(BF16))
            masks.append(band & (cur | (i > 0)))
        chains = [(j, hsel) for j in range(units) for hsel in (head0, ~head0)]
        sc = [jnp.where(masks[j], _bdot_nt(jnp.where(hsel, qs[j], jnp.zeros_like(qs[j])), kcs[j]),
                        NEG_INF) for j, hsel in chains]
        mx = [jnp.max(x, axis=-1, keepdims=True) for x in sc]
        ex = [jnp.exp(x - m) for x, m in zip(sc, mx)]
        den = [jnp.sum(e, axis=-1, keepdims=True) for e in ex]
        pv = [_bdot(e, vcs[j]) for e, (j, _) in zip(ex, chains)]
        for j in range(units):
            a, b = 2 * j, 2 * j + 1
            o_ref[rows(starts[j]), :] = jnp.where(head0, pv[a] / den[a], pv[b] / den[b])
            lse_ref[rows(starts[j]), :] = jnp.where(
                head0, jnp.broadcast_to(mx[a] + jnp.log(den[a]), (A_BLK, LANES)),
                jnp.broadcast_to(mx[b] + jnp.log(den[b]), (A_BLK, LANES)))
        return carry

    lax.fori_loop(0, dil * nb // units, step, 0)


def attn_prompt(qkv, g, layer, n_layers, prev_kv, cos, sin, units=4):
    window, dil = A_GROUPS[g]
    b, s, width = qkv.shape
    assert window == A_BLK * dil and s % window == 0 and (s // A_BLK) % units == 0
    nslab = A_GW // LANES
    assert width == 9 * A_GW
    aliased = prev_kv is not None

    def col_spec(c):
        return pl.BlockSpec((None, s, LANES), lambda i, p: (i, 0, (g * 3 + c) * nslab + p))

    tab_spec = pl.BlockSpec((s, LANES), lambda i, p: (0, 0))
    out_spec = pl.BlockSpec((None, s, LANES), lambda i, p: (i, 0, p))
    in_specs = [col_spec(0), col_spec(1), col_spec(2), tab_spec, tab_spec]
    args = [qkv, qkv, qkv, cos, sin]
    if aliased:
        in_specs.insert(0, pl.BlockSpec(memory_space=pl.ANY))
        args.insert(0, prev_kv)
    return pl.pallas_call(
        functools.partial(_attn_prompt_kernel, s=s, dil=dil, keep=window, units=units, aliased=aliased),
        grid=(b, nslab),
        in_specs=in_specs,
        out_specs=[out_spec, out_spec,
                   pl.BlockSpec((None, None, 2, LANES, window), lambda i, p: (layer, i, 0, p, 0))],
        out_shape=[jax.ShapeDtypeStruct((b, s, A_GW), F32),
                   jax.ShapeDtypeStruct((b, s, A_GW), F32),
                   jax.ShapeDtypeStruct((n_layers, b, 2, A_GW, window), F32)],
        scratch_shapes=[pltpu.VMEM((s, LANES), F32), pltpu.VMEM((s, LANES), F32)],
        input_output_aliases={0: 2} if aliased else {},
        compiler_params=_cparams("parallel", "parallel"),
        name=f"attn_prompt_g{g}",
    )(*args)


def _norm_mm_t_kernel(wt_ref, x_ref, g_ref, o_ref):
    h = _rms(x_ref[...], g_ref[...])
    o_ref[...] = _bdot_nt(wt_ref[...], h)


def norm_mm_t(wt, x, g, tn=768):
    n, d = wt.shape
    m = x.shape[0]
    tn = _row_tile(n, tn)
    return pl.pallas_call(
        _norm_mm_t_kernel,
        grid=(n // tn,),
        in_specs=[pl.BlockSpec((tn, d), lambda i: (i, 0)),
                  pl.BlockSpec((m, d), lambda i: (0, 0)),
                  pl.BlockSpec((1, d), lambda i: (0, 0))],
        out_specs=pl.BlockSpec((tn, m), lambda i: (i, 0)),
        out_shape=jax.ShapeDtypeStruct((n, m), F32),
        compiler_params=_cparams("parallel"),
        name="norm_mm_t",
    )(wt, x, g.reshape(1, d))


def _attn_sample_kernel(*refs, bb, **static):
    if static["aliased"]:
        refs = refs[1:]
    for bl in range(bb):
        _attn_sample_one(bl, pl.program_id(0) * bb + bl, *refs, **static)


def _attn_sample_one(bl, b, q_ref, kt_ref, vt_ref, cos_ref, sin_ref, cost_ref, sint_ref, cache_ref,
                     o_ref, lse_ref, out_ref, *, t_new, length, window, dil, aliased):
    per_blk = LANES // t_new
    new0 = LANES - t_new
    shift = new0 - (b % per_blk) * t_new
    half = HD_A // 2
    ncol = length // LANES
    scale = HD_A ** -0.5
    cos, sin = cos_ref[...], sin_ref[...]
    cos_t, sin_t = cost_ref[...], sint_ref[...]
    q = jnp.concatenate(
        [_rope_slab(q_ref[bl, :, p * LANES:(p + 1) * LANES], cos, sin) for p in range(A_GW // LANES)],
        axis=1) * scale
    q8 = jnp.concatenate([q, jnp.zeros((8 - t_new, A_GW), F32)], axis=0)
    lane = _iota((HD_A, LANES), 1)
    is_new = lane >= new0
    t_b = _iota((8, length), 0)
    l_b = _iota((8, length), 1)
    valid_b = ((l_b & (dil - 1)) == (t_b & (dil - 1))) & (l_b >= t_b + (length - window))
    t_n = _iota((8, LANES), 0)
    u_n = _iota((8, LANES), 1) - new0
    valid_n = (u_n >= 0) & (u_n <= t_n) & (((t_n - u_n) & (dil - 1)) == 0)
    heads = range(HG_A)
    k_new, v_new = [], []
    for h in heads:
        hr = slice(h * HD_A, (h + 1) * HD_A)
        kt = kt_ref[hr, :]
        x1, x2 = kt[:half], kt[half:]
        kt = jnp.concatenate([x1 * cos_t - x2 * sin_t, x2 * cos_t + x1 * sin_t], axis=0)
        k_new.append(jnp.where(is_new, pltpu.roll(kt, shift, 1), 0.0))
        v_new.append(jnp.where(is_new, pltpu.roll(vt_ref[hr, :], shift, 1), 0.0))
    q_h = [q8[:, h * HD_A:(h + 1) * HD_A] for h in heads]
    s_b = [jnp.where(valid_b, _bdot(q_h[h], cache_ref[0, bl, 0, h]), NEG_INF) for h in heads]
    s_n = [jnp.where(valid_n, _bdot(q_h[h], k_new[h]), NEG_INF) for h in heads]
    m = [jnp.maximum(jnp.max(s_b[h], axis=-1, keepdims=True), jnp.max(s_n[h], axis=-1, keepdims=True))
         for h in heads]
    e_b = [jnp.exp(s_b[h] - m[h]) for h in heads]
    e_n = [jnp.exp(s_n[h] - m[h]) for h in heads]
    l = [jnp.sum(e_b[h], axis=-1, keepdims=True) + jnp.sum(e_n[h], axis=-1, keepdims=True) for h in heads]
    o = [(_bdot_nt(e_b[h], cache_ref[0, bl, 1, h]) + _bdot_nt(e_n[h], v_new[h])) / l[h] for h in heads]
    o_ref[bl] = jnp.concatenate(o, axis=1)[:t_new]
    lse_ref[bl] = jnp.concatenate(
        [jnp.broadcast_to(m[h] + jnp.log(l[h]), (8, HD_A)) for h in heads], axis=1)[:t_new]
    for h in heads:
        for c, new in enumerate((k_new[h], v_new[h])):
            rolled = [pltpu.roll(cache_ref[0, bl, c, h, :, i * LANES:(i + 1) * LANES], new0, 1)
                      for i in range(ncol)]
            for i in range(ncol):
                nxt = rolled[i + 1] if i + 1 < ncol else new
                out_ref[0, bl, c, h, :, i * LANES:(i + 1) * LANES] = jnp.where(is_new, nxt, rolled[i])


def attn_sample(g, layer, qkv, kvt_new, tabs, tabs_t, cache_t, prev_out):
    window, dil = A_GROUPS[g]
    bd_, t_new, width = qkv.shape
    n_layers, _, _, _, _, length = cache_t.shape
    assert LANES % t_new == 0 and length % LANES == 0 and t_new <= 8 and dil & (dil - 1) == 0
    per_blk = LANES // t_new
    slabs = width // A_GW
    kv_rows = kvt_new.shape[0] // A_GW
    aliased = prev_out is not None
    seq_bytes = 2 * HG_A * HD_A * length * 4
    bb = max(1, min(SAMPLE_SEQS_PER_STEP, VMEM_LIMIT // 2 // (4 * seq_bytes)))
    assert per_blk % bb == 0 and bd_ % bb == 0
    in_specs = [pl.BlockSpec((bb, t_new, A_GW), lambda i: (i, 0, g * 3)),
                pl.BlockSpec((A_GW, LANES), lambda i: (2 * g, i * bb // per_blk)),
                pl.BlockSpec((A_GW, LANES), lambda i: (2 * g + 1, i * bb // per_blk)),
                pl.BlockSpec((t_new, LANES), lambda i: (0, 0)),
                pl.BlockSpec((t_new, LANES), lambda i: (0, 0)),
                pl.BlockSpec((HD_A // 2, LANES), lambda i: (0, 0)),
                pl.BlockSpec((HD_A // 2, LANES), lambda i: (0, 0)),
                pl.BlockSpec((1, bb, 2, HG_A, HD_A, length), lambda i: (layer, i, 0, 0, 0, 0))]
    args = [qkv, kvt_new, kvt_new, *tabs, *tabs_t, cache_t]
    if aliased:
        in_specs.insert(0, pl.BlockSpec(memory_space=pl.ANY))
        args.insert(0, prev_out)
    assert slabs == 3 * len(A_GROUPS) and kv_rows == 2 * len(A_GROUPS)
    return pl.pallas_call(
        functools.partial(_attn_sample_kernel, bb=bb, t_new=t_new, length=length, window=window,
                          dil=dil, aliased=aliased),
        grid=(bd_ // bb,),
        in_specs=in_specs,
        out_specs=[pl.BlockSpec((bb, t_new, A_GW), lambda i: (i, 0, 0)),
                   pl.BlockSpec((bb, t_new, A_GW), lambda i: (i, 0, 0)),
                   pl.BlockSpec((1, bb, 2, HG_A, HD_A, length), lambda i: (layer, i, 0, 0, 0, 0))],
        out_shape=[jax.ShapeDtypeStruct((bd_, t_new, A_GW), F32),
                   jax.ShapeDtypeStruct((bd_, t_new, A_GW), F32),
                   jax.ShapeDtypeStruct(cache_t.shape, F32)],
        input_output_aliases={0: 2} if aliased else {},
        compiler_params=_cparams("arbitrary"),
        name=f"attn_sample_g{g}",
    )(*args)


def _combine_out_kernel(*refs, ng):
    o_refs, l_refs = refs[:ng], refs[ng:2 * ng]
    w_ref, r_ref, out_ref = refs[2 * ng:]
    ls = [l[...] for l in l_refs]
    m = functools.reduce(jnp.maximum, ls)
    es = [jnp.exp(l - m) for l in ls]
    inv = 1.0 / functools.reduce(jnp.add, es)
    a = jnp.concatenate([o[...] * (e * inv) for o, e in zip(o_refs, es)], axis=1)
    out_ref[...] = r_ref[...] + _bdot(a, w_ref[...])


def combine_out(os_, lses, w, r, tm=512):
    ng = len(os_)
    m = os_[0].shape[0]
    gw = A_GW
    d = w.shape[1]
    tm = _row_tile(m, tm)
    slabs = [pl.BlockSpec((tm, gw), lambda i, c=(g if o.shape[1] > gw else 0): (i, c))
             for g, o in enumerate(os_)]
    return pl.pallas_call(
        functools.partial(_combine_out_kernel, ng=ng),
        grid=(m // tm,),
        in_specs=slabs * 2 + [pl.BlockSpec((ng * gw, d), lambda i: (0, 0)),
                              pl.BlockSpec((tm, d), lambda i: (i, 0))],
        out_specs=pl.BlockSpec((tm, d), lambda i: (i, 0)),
        out_shape=jax.ShapeDtypeStruct((m, d), F32),
        compiler_params=_cparams("parallel"),
        name="combine_out",
    )(*os_, *lses, w, r)


def _split3_dot(a01, b):
    a = a01.astype(BF16)
    b1 = b.astype(BF16)
    r1 = b - b1.astype(F32)
    b2 = r1.astype(BF16)
    b3 = (r1 - b2.astype(F32)).astype(BF16)
    dot = functools.partial(jnp.dot, preferred_element_type=F32)
    return dot(a, b1) + dot(a, b2) + dot(a, b3)


def _delta_kernel(q_ref, k_ref, v_ref, z_ref, gt_ref, hq_ref, hk_ref, hv_ref, cw_q, cw_k, cw_v,
                  alog_ref, dtb_ref, nw_ref, s0_ref, o_ref, s_ref, xq, xk, xv,
                  *, t_len, t_valid, chunk, cg, hb, n_heads):
    halo = 8
    for src, hal, dst in ((q_ref, hq_ref, xq), (k_ref, hk_ref, xk), (v_ref, hv_ref, xv)):
        dst[0:halo, :] = hal[0]
        dst[halo:, :] = src[0]
    rows = cg * chunk
    ci = _iota((chunk, chunk), 0)
    si = _iota((chunk, chunk), 1)
    ltri = (ci >= si).astype(F32)
    strict = ci > si
    eye = (ci == si).astype(F32)
    merge_masks = []
    for lg in range(chunk.bit_length() - 1):
        merge_masks.append(((ci >> (lg + 1)) == (si >> (lg + 1)))
                           & (((ci >> lg) & 1) == 1) & (((si >> lg) & 1) == 0))
    lane_r = _iota((rows, LANES), 1)
    lane_c = _iota((chunk, LANES), 1)
    nw = nw_ref[...]
    head_base = pl.program_id(1) * hb

    def conv_act(xs, cw, r0, cols):
        win = xs[pl.ds(r0, rows + halo), cols]
        w = cw[:, cols]
        c = win[halo - (CONV_W - 1):halo - (CONV_W - 1) + rows, :] * w[0:1, :]
        for j in range(1, CONV_W):
            off = halo - (CONV_W - 1) + j
            c = c + win[off:off + rows, :] * w[j:j + 1, :]
        return _silu(c)

    def l2n(x):
        return x * lax.rsqrt(jnp.sum(x * x, axis=-1, keepdims=True) + L2_EPS)

    def lane_col(x, lane_ids, idx):
        return jnp.sum(jnp.where(lane_ids == idx, x, 0.0), axis=-1, keepdims=True)

    def group(i, states):
        r0 = pl.multiple_of(i * rows, rows)
        gate = gt_ref[0, pl.ds(r0, rows), :]
        x = gate + dtb_ref[...]
        softplus = jnp.maximum(x, 0.0) + jnp.log(1.0 + jnp.exp(-jnp.abs(x)))
        g_all = -jnp.exp(alog_ref[...]) * softplus
        beta_all = 1.0 / (1.0 + jnp.exp(-gate))
        if t_valid < t_len:
            valid = (r0 + _iota((rows, 1), 0)) < t_valid
            g_all = jnp.where(valid, g_all, 0.0)
            beta_all = jnp.where(valid, beta_all, 0.0)
        chunks = [slice(c * chunk, (c + 1) * chunk) for c in range(cg)]
        g_cum_all = [_split3_dot(ltri, g_all[rs]) for rs in chunks]
        heads = []
        for hh in range(hb):
            cols = slice(hh * DK_B, (hh + 1) * DK_B)
            k_all = l2n(conv_act(xk, cw_k, r0, cols))
            if t_valid < t_len:
                k_all = jnp.where(valid, k_all, 0.0)
            heads.append((l2n(conv_act(xq, cw_q, r0, cols)) * (DK_B ** -0.5), k_all,
                          conv_act(xv, cw_v, r0, cols),
                          lane_col(beta_all, lane_r, n_heads + head_base + hh)))
        units = [(hh, c) for c in range(cg) for hh in range(hb)]
        q_, k_, v_, kb_, gc_, dec_ = [], [], [], [], [], []
        for hh, c in units:
            q_all, k_all, v_all, beta_col = heads[hh]
            rs = chunks[c]
            g_c = jnp.broadcast_to(lane_col(g_cum_all[c], lane_c, head_base + hh), (chunk, LANES))
            g_r = jnp.transpose(jnp.concatenate(
                [g_c, jnp.zeros((LANES - chunk, LANES), F32)], axis=0))[:chunk, :chunk]
            diff = jnp.where(ci >= si, g_c[:, :chunk] - g_r, 0.0)
            q_.append(q_all[rs]); k_.append(k_all[rs]); v_.append(v_all[rs] * beta_col[rs])
            kb_.append(k_all[rs] * beta_col[rs]); gc_.append(g_c)
            dec_.append(jnp.where(ci >= si, jnp.exp(diff), 0.0))
        n_u = len(units)
        a_ = [jnp.where(strict, _bdot_nt(kb_[u], k_[u]) * dec_[u], 0.0) for u in range(n_u)]
        t_ = [eye - jnp.where(merge_masks[0], a_[u], 0.0) for u in range(n_u)]
        for mask in merge_masks[1:]:
            low = [_bdot(jnp.where(mask, a_[u], 0.0), t_[u]) for u in range(n_u)]
            t_ = [t_[u] - _bdot(t_[u], low[u]) for u in range(n_u)]
        eg_ = [jnp.exp(gc_[u]) for u in range(n_u)]
        uw_ = [_bdot(t_[u], jnp.concatenate([v_[u], kb_[u] * eg_[u]], axis=1)) for u in range(n_u)]
        attn_ = [_bdot_nt(q_[u], k_[u]) * dec_[u] for u in range(n_u)]
        states = list(states)
        for c in range(cg):
            us = [u for u in range(n_u) if units[u][1] == c]
            glast = {u: gc_[u][chunk - 1:chunk, :] for u in us}
            ws = {u: _bdot(uw_[u][:, DK_B:], states[units[u][0]]) for u in us}
            qs = {u: _bdot(q_[u] * eg_[u], states[units[u][0]]) for u in us}
            vn = {u: uw_[u][:, :DK_B] - ws[u] for u in us}
            for u in us:
                hh = units[u][0]
                kd = k_[u] * jnp.exp(glast[u] - gc_[u])
                o = qs[u] + _bdot(attn_[u], vn[u])
                states[hh] = states[hh] * jnp.exp(glast[u]) + _bdot_tn(kd, vn[u])
                o = o * lax.rsqrt(jnp.mean(o * o, axis=-1, keepdims=True) + RMS_EPS) * nw
                cols = slice(hh * DK_B, (hh + 1) * DK_B)
                z = z_ref[0, pl.ds(r0 + c * chunk, chunk), cols]
                o_ref[0, pl.ds(r0 + c * chunk, chunk), cols] = o * _silu(z)
        return tuple(states)

    init = tuple(s0_ref[0, hh] for hh in range(hb))
    final = lax.fori_loop(0, t_len // rows, group, init)
    for hh in range(hb):
        s_ref[0, hh] = final[hh]


def delta_core(proj, conv_halo, s0, conv_w, a_log, dt_bias, norm_w, *, t_valid, chunk, cg, hb):
    b, t_len, width = proj.shape
    n_heads = s0.shape[1]
    hw = n_heads * DK_B
    assert width == 4 * hw + LANES and t_len % (cg * chunk) == 0 and n_heads % hb == 0
    nhb = n_heads // hb
    bw = hb * DK_B

    def col_spec(rows, c):
        return pl.BlockSpec((1, rows, bw), lambda i, h: (i, 0, c * nhb + h))

    def row_vec(x):
        return jnp.pad(x.astype(F32), (0, LANES - x.shape[0])).reshape(1, LANES)

    vec_spec = pl.BlockSpec((1, LANES), lambda i, h: (0, 0))
    cw_specs = [pl.BlockSpec((CONV_W, bw), lambda i, h, c=c: (0, c * nhb + h)) for c in range(3)]
    return pl.pallas_call(
        functools.partial(_delta_kernel, t_len=t_len, t_valid=t_valid, chunk=chunk, cg=cg,
                          hb=hb, n_heads=n_heads),
        grid=(b, nhb),
        in_specs=[col_spec(t_len, 0), col_spec(t_len, 1), col_spec(t_len, 2), col_spec(t_len, 3),
                  pl.BlockSpec((1, t_len, LANES), lambda i, h: (i, 0, 4 * hw // LANES)),
                  col_spec(8, 0), col_spec(8, 1), col_spec(8, 2)] + cw_specs
                 + [vec_spec, vec_spec, vec_spec,
                    pl.BlockSpec((1, hb, DK_B, DK_B), lambda i, h: (i, h, 0, 0))],
        out_specs=[pl.BlockSpec((1, t_len, bw), lambda i, h: (i, 0, h)),
                   pl.BlockSpec((1, hb, DK_B, DK_B), lambda i, h: (i, h, 0, 0))],
        out_shape=[jax.ShapeDtypeStruct((b, t_len, hw), F32),
                   jax.ShapeDtypeStruct(s0.shape, F32)],
        scratch_shapes=[pltpu.VMEM((t_len + 8, bw), F32)] * 3,
        compiler_params=_cparams("parallel", "parallel"),
        name="delta_core",
    )(proj, proj, proj, proj, proj, conv_halo, conv_halo, conv_halo, conv_w, conv_w, conv_w,
      row_vec(a_log), row_vec(dt_bias), norm_w.astype(F32).reshape(1, DK_B), s0)


def _pool_prompt_kernel(x_ref, xh_ref, g_ref, w_ref, sc_ref, o_ref, tail_ref, hs, *, tm):
    i = pl.program_id(1)
    g = g_ref[...]
    x = x_ref[0]
    h = _rms(x, g)
    hs[0:16, :] = jnp.where(i > 0, _rms(xh_ref[0], g), 0.0)
    hs[16:, :] = h
    tail_ref[0] = h[tm - 16:, :]
    pos = (i * tm + _iota((tm, 1), 0) + 1).astype(F32)
    cg = x.shape[1] // len(POOL_WINDOWS)
    for gi, w in enumerate(POOL_WINDOWS):
        cols = slice(gi * cg, (gi + 1) * cg)
        tot = h[:, cols]
        for j in range(1, w):
            tot = tot + hs[16 - j:16 - j + tm, cols]
        d = tot / jnp.minimum(float(w), pos) - h[:, cols]
        o_ref[0, :, cols] = x[:, cols] + _bdot(d, w_ref[gi]) * sc_ref[:, cols]


def pool_prompt(x, g, w_grp, scale, tm=512):
    b, s, d = x.shape
    tm = _row_tile(s, tm)
    per = tm // 16
    ng, cgw, _ = w_grp.shape
    return pl.pallas_call(
        functools.partial(_pool_prompt_kernel, tm=tm),
        grid=(b, s // tm),
        in_specs=[pl.BlockSpec((1, tm, d), lambda i, j: (i, j, 0)),
                  pl.BlockSpec((1, 16, d), lambda i, j: (i, jnp.maximum(j * per - 1, 0), 0)),
                  pl.BlockSpec((1, d), lambda i, j: (0, 0)),
                  pl.BlockSpec((ng, cgw, cgw), lambda i, j: (0, 0, 0)),
                  pl.BlockSpec((1, d), lambda i, j: (0, 0))],
        out_specs=[pl.BlockSpec((1, tm, d), lambda i, j: (i, j, 0)),
                   pl.BlockSpec((1, 16, d), lambda i, j: (i, 0, 0))],
        out_shape=[jax.ShapeDtypeStruct((b, s, d), F32),
                   jax.ShapeDtypeStruct((b, 16, d), F32)],
        scratch_shapes=[pltpu.VMEM((tm + 16, d), F32)],
        compiler_params=_cparams("parallel", "arbitrary"),
        name="pool_prompt",
    )(x, x, g.reshape(1, d), w_grp, scale.reshape(1, d))


def _pool_sample_kernel(x_ref, pre_ref, g_ref, w_ref, sc_ref, o_ref, h_ref, *, t_new, n_pre):
    g = g_ref[...]
    xs = [x_ref[t] for t in range(t_new)]
    rows = [pre_ref[r] for r in range(n_pre)] + [_rms(x, g) for x in xs]
    cg = xs[0].shape[1] // len(POOL_WINDOWS)
    for t in range(t_new):
        h_ref[t] = rows[n_pre + t]
        for gi, w in enumerate(POOL_WINDOWS):
            cols = slice(gi * cg, (gi + 1) * cg)
            lo = max(n_pre + t - w + 1, 0)
            tot = rows[lo][:, cols]
            for r in range(lo + 1, n_pre + t + 1):
                tot = tot + rows[r][:, cols]
            d = tot / float(min(w, t + 1 + n_pre)) - rows[n_pre + t][:, cols]
            o_ref[t, :, cols] = xs[t][:, cols] + _bdot(d, w_ref[gi]) * sc_ref[:, cols]


def pool_sample(x_t, pre_t, g, w_grp, scale):
    t_new, bd_, d = x_t.shape
    n_pre = pre_t.shape[0]
    ng, cgw, _ = w_grp.shape
    full = lambda shape: pl.BlockSpec(shape, lambda i: (0,) * len(shape))
    return pl.pallas_call(
        functools.partial(_pool_sample_kernel, t_new=t_new, n_pre=n_pre),
        grid=(1,),
        in_specs=[full((t_new, bd_, d)), full((n_pre, bd_, d)), full((1, d)),
                  full((ng, cgw, cgw)), full((1, d))],
        out_specs=[full((t_new, bd_, d)), full((t_new, bd_, d))],
        out_shape=[jax.ShapeDtypeStruct((t_new, bd_, d), F32),
                   jax.ShapeDtypeStruct((t_new, bd_, d), F32)],
        compiler_params=_cparams("arbitrary"),
        name="pool_sample",
    )(x_t, pre_t, g.reshape(1, d), w_grp, scale.reshape(1, d))


def _rope_tables_t(pos, n_cols):
    half = HD_A // 2
    inv = ROPE_THETA ** (-jnp.arange(half, dtype=F32) / half)
    ang = inv[:, None] * jnp.tile(pos.astype(F32), n_cols // pos.shape[0])[None, :]
    return jnp.cos(ang), jnp.sin(ang)


def _mixer_a(xp, xs, g1, w_in, wt_kv, w_out, layer, caches_t, prev_kv_p, prev_outs,
             tabs_p, tabs_s, tabs_st):
    bp, sp, d = xp.shape
    bs, ts, _ = xs.shape
    n_layers = caches_t[0].shape[0]
    xp2, xs2 = xp.reshape(-1, d), xs.reshape(-1, d)
    qkv_p = norm_mm(xp2, g1, w_in).reshape(bp, sp, -1)
    qkv_s = norm_mm(xs2, g1, w_in).reshape(bs, ts, -1)
    kvt_new = norm_mm_t(wt_kv, xs2, g1)
    os_, lses, kv_p = [], [], []
    os_s, lses_s, kv_s = [], [], []
    for g, (window, dil) in enumerate(A_GROUPS):
        assert sp >= window and caches_t[g].shape[-1] >= window
        o, lse, kv = attn_prompt(qkv_p, g, layer, n_layers,
                                 None if prev_kv_p is None else prev_kv_p[g], *tabs_p)
        os_.append(o.reshape(-1, A_GW))
        lses.append(lse.reshape(-1, A_GW))
        kv_p.append(kv)
        o, lse, out = attn_sample(g, layer, qkv_s, kvt_new, tabs_s, tabs_st, caches_t[g],
                                  None if prev_outs is None else prev_outs[g])
        os_s.append(o.reshape(-1, A_GW))
        lses_s.append(lse.reshape(-1, A_GW))
        kv_s.append(out)
    yp = combine_out(os_, lses, w_out, xp2).reshape(xp.shape)
    ys = combine_out(os_s, lses_s, w_out, xs2).reshape(xs.shape)
    return yp, ys, kv_p, kv_s


def _mixer_b(xp, xs, g1, w_in, conv_w, a_log, dt_bias, norm_w, w_out, conv_s0, state_s0):
    bp, sp, d = xp.shape
    bs, ts, _ = xs.shape
    n_heads = state_s0.shape[1]
    qkv_w = 3 * n_heads * DK_B
    xp2, xs2 = xp.reshape(-1, d), xs.reshape(-1, d)
    proj_p = norm_mm(xp2, g1, w_in).reshape(bp, sp, -1)
    proj_s = norm_mm(xs2, g1, w_in).reshape(bs, ts, -1)
    consts = (conv_w, a_log, dt_bias, norm_w)
    o_p, st_p = delta_core(proj_p, jnp.zeros((bp, 8, qkv_w), F32),
                           jnp.zeros((bp,) + state_s0.shape[1:], F32), *consts,
                           t_valid=sp, chunk=CHUNK_B, cg=8, hb=2)
    yp = mm_res(o_p.reshape(bp * sp, -1), w_out, xp2).reshape(xp.shape)
    t_pad = 8
    assert CONV_W - 1 <= ts <= t_pad
    o_s, st_s = delta_core(jnp.pad(proj_s, ((0, 0), (0, t_pad - ts), (0, 0))),
                           jnp.pad(conv_s0, ((0, 0), (8 - (CONV_W - 1), 0), (0, 0))),
                           state_s0, *consts, t_valid=ts, chunk=t_pad, cg=1, hb=n_heads)
    ys = mm_res(o_s[:, :ts].reshape(bs * ts, -1), w_out, xs2).reshape(xs.shape)
    conv_p = proj_p[:, sp - (CONV_W - 1):, :qkv_w]
    conv_s = proj_s[:, ts - (CONV_W - 1):, :qkv_w]
    return yp, ys, (st_p, conv_p), (st_s, conv_s)


def _mixer_c(xp, xs, g1, w_grp, scale, pool_s0):
    ts = xs.shape[1]
    yp, tail = pool_prompt(xp, g1, w_grp, scale)
    pool_p = tail[:, tail.shape[1] - POOL_BUF:]
    y_t, h_t = pool_sample(xs.transpose(1, 0, 2), pool_s0.transpose(1, 0, 2), g1, w_grp, scale)
    pool_s = jnp.concatenate([pool_s0, h_t.transpose(1, 0, 2)], axis=1)[:, ts:]
    return yp, y_t.transpose(1, 0, 2), pool_p, pool_s


def kernel(x_prompt, x_sample, cache_a_kv1, cache_a_kv2, cache_a_kv3, state_b_s, state_b_conv, state_c_pool, norm1, norm2, norm_f, a_w_in, a_w_out, b_w_in, b_conv, b_a_log, b_dt_bias, b_norm, b_w_out, c_w, c_scale, mlp_w1, mlp_w2):
    xp, xs = x_prompt, x_sample
    depth = norm1.shape[0]
    d = xp.shape[-1]
    past_len = cache_a_kv3.shape[2]
    tabs_p = _rope_tables(jnp.arange(xp.shape[1]))
    pos_s = past_len + jnp.arange(xs.shape[1])
    tabs_s = _rope_tables(pos_s)
    tabs_st = _rope_tables_t(pos_s, LANES)
    caches_t = [jnp.transpose(c, (0, 1, 3, 4, 5, 2)) for c in (cache_a_kv1, cache_a_kv2, cache_a_kv3)]
    kv_cols = np.concatenate([np.arange(g * 3 * A_GW + A_GW, (g + 1) * 3 * A_GW)
                              for g in range(len(A_GROUPS))])
    b_pad = (-b_w_in.shape[2]) % LANES
    b_p, b_s, c_p, c_s = [], [], [], []
    kv_p = kv_s = None
    for i in range(depth):
        kind, j = i % 3, i // 3
        if kind == 0:
            w_in = a_w_in[j].astype(BF16)
            xp, xs, kv_p, kv_s = _mixer_a(xp, xs, norm1[i], w_in, w_in[:, kv_cols].T,
                                          a_w_out[j].astype(BF16), j, caches_t, kv_p, kv_s,
                                          tabs_p, tabs_s, tabs_st)
        elif kind == 1:
            w_in = jnp.pad(b_w_in[j], ((0, 0), (0, b_pad))).astype(BF16)
            xp, xs, st_p, st_s = _mixer_b(xp, xs, norm1[i], w_in, b_conv[j], b_a_log[j], b_dt_bias[j],
                                          b_norm[j], b_w_out[j].astype(BF16), state_b_conv[j], state_b_s[j])
            b_p.append(st_p)
            b_s.append(st_s)
        else:
            xp, xs, pool_p, pool_s = _mixer_c(xp, xs, norm1[i], c_w[j].astype(BF16), c_scale[j],
                                              state_c_pool[j])
            c_p.append(pool_p)
            c_s.append(pool_s)
        w1, w2 = mlp_w1[i].astype(BF16), mlp_w2[i].astype(BF16)
        last = i == depth - 1
        xp = mlp_res(xp.reshape(-1, d), norm2[i], w1, w2, norm_f, last).reshape(xp.shape)
        xs = mlp_res(xs.reshape(-1, d), norm2[i], w1, w2, norm_f, last).reshape(xs.shape)
    stack = lambda items, k: jnp.stack([e[k] for e in items])
    bp = xp.shape[0]
    kv_p = [jnp.transpose(c.reshape(c.shape[0], bp, 2, HG_A, HD_A, c.shape[-1]), (0, 1, 5, 2, 3, 4))
            for c in kv_p]
    return (xp, xs, *kv_p,
            stack(b_p, 0), stack(b_p, 1), jnp.stack(c_p),
            *[jnp.transpose(c, (0, 1, 5, 2, 3, 4)) for c in kv_s],
            stack(b_s, 0), stack(b_s, 1), jnp.stack(c_s))
```

```python
import functools
import math

import numpy as np
import jax
import jax.numpy as jnp
from jax import lax
from jax.experimental import pallas as pl
from jax.experimental.pallas import tpu as pltpu

F32 = jnp.float32
BF16 = jnp.bfloat16

RMS_EPS = 1e-6
L2_EPS = 1e-6
ROPE_THETA = 10000.0
A_GROUPS = ((128, 1), (512, 4), (2048, 16))
HG_A = 6
HD_A = 64
A_GW = HG_A * HD_A
A_BLK = 128
CONV_W = 4
DK_B = 128
CHUNK_B = 64
POOL_WINDOWS = (2, 4, 8, 16)
POOL_BUF = max(POOL_WINDOWS) - 1
LANES = 128
VMEM_LIMIT = 56 * 1024 * 1024
SAMPLE_SEQS_PER_STEP = 4
NEG_INF = float("-inf")


def _cparams(*sem):
    return pltpu.CompilerParams(dimension_semantics=sem, vmem_limit_bytes=VMEM_LIMIT)


def _bdot(a, b):
    return jnp.dot(a.astype(BF16), b.astype(BF16), preferred_element_type=F32)


def _bdot_nt(a, b):
    return lax.dot_general(a.astype(BF16), b.astype(BF16), (((1,), (1,)), ((), ())),
                           preferred_element_type=F32)


def _bdot_tn(a, b):
    return lax.dot_general(a.astype(BF16), b.astype(BF16), (((0,), (0,)), ((), ())),
                           preferred_element_type=F32)


def _rms(x, g):
    return x * lax.rsqrt(jnp.mean(x * x, axis=-1, keepdims=True) + RMS_EPS) * g


def _silu(x):
    return x * (1.0 / (1.0 + jnp.exp(-x)))


def _iota(shape, dim):
    return lax.broadcasted_iota(jnp.int32, shape, dim)


def _row_tile(m, want):
    t = min(m, want)
    assert m % t == 0, (m, t)
    return t


def _norm_mm_kernel(x_ref, g_ref, w_ref, o_ref):
    h = _rms(x_ref[...], g_ref[...])
    o_ref[...] = _bdot(h, w_ref[...])


def norm_mm(x, g, w, tm=512):
    m, d = x.shape
    n = w.shape[1]
    tm = _row_tile(m, tm)
    return pl.pallas_call(
        _norm_mm_kernel,
        grid=(m // tm,),
        in_specs=[pl.BlockSpec((tm, d), lambda i: (i, 0)),
                  pl.BlockSpec((1, d), lambda i: (0, 0)),
                  pl.BlockSpec((d, n), lambda i: (0, 0))],
        out_specs=pl.BlockSpec((tm, n), lambda i: (i, 0)),
        out_shape=jax.ShapeDtypeStruct((m, n), F32),
        compiler_params=_cparams("parallel"),
        name="norm_mm",
    )(x, g.reshape(1, d), w)


def _mlp_body(x, g_ref, w1_ref, w2_ref, gf_ref, o_ref, fc, final_norm):
    h = _rms(x, g_ref[...]).astype(BF16)
    acc = x
    for c in range(w1_ref.shape[1] // fc):
        a = jnp.dot(h, w1_ref[:, c * fc:(c + 1) * fc], preferred_element_type=F32)
        a = jnp.square(jnp.maximum(a, 0.0))
        acc = acc + _bdot(a, w2_ref[c * fc:(c + 1) * fc, :])
    if final_norm:
        acc = _rms(acc, gf_ref[...])
    o_ref[...] = acc


def _group_softmax_concat(o_refs, l_refs):
    ls = [l[...] for l in l_refs]
    m = functools.reduce(jnp.maximum, ls)
    es = [jnp.exp(l - m) for l in ls]
    inv = 1.0 / functools.reduce(jnp.add, es)
    return jnp.concatenate([o[...] * (e * inv) for o, e in zip(o_refs, es)], axis=1)


def _proj_mlp_kernel(*refs, ng, fc, final_norm):
    if ng:
        a = _group_softmax_concat(refs[:ng], refs[ng:2 * ng])
        refs = refs[2 * ng:]
    else:
        a = refs[0][...]
        refs = refs[1:]
    wo_ref, r_ref, g_ref, w1_ref, w2_ref, gf_ref, o_ref = refs
    x = r_ref[...] + _bdot(a, wo_ref[...])
    _mlp_body(x, g_ref, w1_ref, w2_ref, gf_ref, o_ref, fc, final_norm)


def proj_mlp_res(parts, lses, w_out, r, g, w1, w2, gf, final_norm, tm=512, fc=1024):
    ng = len(lses)
    m, d = r.shape
    f = w1.shape[1]
    tm = _row_tile(m, tm)
    const = dict(pipeline_mode=pl.Buffered(1))
    part_specs = [pl.BlockSpec((tm, p.shape[1]), lambda i: (i, 0)) for p in list(parts) + list(lses)]
    return pl.pallas_call(
        functools.partial(_proj_mlp_kernel, ng=ng, fc=fc, final_norm=final_norm),
        grid=(m // tm,),
        in_specs=part_specs + [pl.BlockSpec(w_out.shape, lambda i: (0, 0), **const),
                               pl.BlockSpec((tm, d), lambda i: (i, 0)),
                               pl.BlockSpec((1, d), lambda i: (0, 0), **const),
                               pl.BlockSpec((d, f), lambda i: (0, 0), **const),
                               pl.BlockSpec((f, d), lambda i: (0, 0), **const),
                               pl.BlockSpec((1, d), lambda i: (0, 0), **const)],
        out_specs=pl.BlockSpec((tm, d), lambda i: (i, 0)),
        out_shape=jax.ShapeDtypeStruct((m, d), F32),
        compiler_params=_cparams("parallel"),
        name="proj_mlp_res",
    )(*parts, *lses, w_out, r, g.reshape(1, d), w1, w2, gf.reshape(1, d))


def _mlp_kernel(x_ref, g_ref, w1_ref, w2_ref, gf_ref, o_ref, *, fc, final_norm):
    _mlp_body(x_ref[...], g_ref, w1_ref, w2_ref, gf_ref, o_ref, fc, final_norm)


def mlp_res(x, g, w1, w2, gf, final_norm, tm=512, fc=1024):
    m, d = x.shape
    f = w1.shape[1]
    tm = _row_tile(m, tm)
    return pl.pallas_call(
        functools.partial(_mlp_kernel, fc=fc, final_norm=final_norm),
        grid=(m // tm,),
        in_specs=[pl.BlockSpec((tm, d), lambda i: (i, 0)),
                  pl.BlockSpec((1, d), lambda i: (0, 0)),
                  pl.BlockSpec((d, f), lambda i: (0, 0)),
                  pl.BlockSpec((f, d), lambda i: (0, 0)),
                  pl.BlockSpec((1, d), lambda i: (0, 0))],
        out_specs=pl.BlockSpec((tm, d), lambda i: (i, 0)),
        out_shape=jax.ShapeDtypeStruct((m, d), F32),
        compiler_params=_cparams("parallel"),
        name="mlp_res",
    )(x, g.reshape(1, d), w1, w2, gf.reshape(1, d))


def _rope_tables(pos):
    half = HD_A // 2
    inv = ROPE_THETA ** (-jnp.arange(half, dtype=F32) / half)
    ang = pos.astype(F32)[:, None] * inv[None, :]
    cos = jnp.tile(jnp.cos(ang), (1, LANES // half))
    sin = jnp.sin(ang)
    sin = jnp.tile(jnp.concatenate([-sin, sin], axis=1), (1, LANES // HD_A))
    return cos, sin


def _rope_slab(x, cos, sin):
    half = HD_A // 2
    first = (_iota(x.shape, 1) % HD_A) < half
    swapped = jnp.where(first, pltpu.roll(x, LANES - half, 1), pltpu.roll(x, half, 1))
    return x * cos + swapped * sin


def _attn_prompt_kernel(*refs, s, dil, keep, units, aliased):
    if aliased:
        refs = refs[1:]
    q_ref, k_ref, v_ref, cos_ref, sin_ref, o_ref, lse_ref, kv_ref, qr, kr = refs
    nb = s // (dil * A_BLK)
    span = A_BLK * dil
    cos, sin = cos_ref[...], sin_ref[...]
    k_rot = _rope_slab(k_ref[...], cos, sin)
    kr[...] = k_rot
    qr[...] = _rope_slab(q_ref[...], cos, sin) * (HD_A ** -0.5)
    kv_ref[0] = jnp.transpose(k_rot[s - keep:, :])
    kv_ref[1] = jnp.transpose(v_ref[s - keep:, :])
    head0 = _iota((A_BLK, LANES), 1) < HD_A
    qi = _iota((A_BLK, 2 * A_BLK), 0)
    kj = _iota((A_BLK, 2 * A_BLK), 1)
    dist = A_BLK + qi - kj
    band = (dist >= 0) & (dist <= A_BLK)
    cur = kj >= A_BLK

    def rows(start):
        if dil == 1:
            return pl.ds(pl.multiple_of(start, A_BLK), A_BLK)
        return pl.ds(start, A_BLK, stride=dil)

    def step(it, carry):
        starts, qs, kcs, vcs, masks = [], [], [], [], []
        for j in range(units):
            u = it * units + j
            r, i = u // nb, u % nb
            start = r + i * span
            prev = jnp.where(i > 0, start - span, start)
            starts.append(start)
            qs.append(qr[rows(start), :].astype(BF16))
            kcs.append(jnp.concatenate([kr[rows(prev), :], kr[rows(start), :]], axis=0).astype(BF16))
            vcs.append(jnp.concatenate([v_ref[rows(prev), :], v_ref[rows(start), :]],
                                       axis=0).astype(BF16))
            masks.append(band & (cur | (i > 0)))
        chains = [(j, hsel) for j in range(units) for hsel in (head0, ~head0)]
        sc = [jnp.where(masks[j], _bdot_nt(jnp.where(hsel, qs[j], jnp.zeros_like(qs[j])), kcs[j]),
                        NEG_INF) for j, hsel in chains]
        mx = [jnp.max(x, axis=-1, keepdims=True) for x in sc]
        ex = [jnp.exp(x - m) for x, m in zip(sc, mx)]
        den = [jnp.sum(e, axis=-1, keepdims=True) for e in ex]
        pv = [_bdot(e, vcs[j]) for e, (j, _) in zip(ex, chains)]
        for j in range(units):
            a, b = 2 * j, 2 * j + 1
            o_ref[rows(starts[j]), :] = jnp.where(head0, pv[a] / den[a], pv[b] / den[b])
            lse_ref[rows(starts[j]), :] = jnp.where(
                head0, jnp.broadcast_to(mx[a] + jnp.log(den[a]), (A_BLK, LANES)),
                jnp.broadcast_to(mx[b] + jnp.log(den[b]), (A_BLK, LANES)))
        return carry

    lax.fori_loop(0, dil * nb // units, step, 0)


def attn_prompt(qkv, g, layer, n_layers, prev_kv, cos, sin, units=4):
    window, dil = A_GROUPS[g]
    b, s, width = qkv.shape
    assert window == A_BLK * dil and s % window == 0 and (s // A_BLK) % units == 0
    nslab = A_GW // LANES
    assert width == 9 * A_GW
    aliased = prev_kv is not None

    def col_spec(c):
        return pl.BlockSpec((None, s, LANES), lambda i, p: (i, 0, (g * 3 + c) * nslab + p))

    tab_spec = pl.BlockSpec((s, LANES), lambda i, p: (0, 0))
    out_spec = pl.BlockSpec((None, s, LANES), lambda i, p: (i, 0, p))
    in_specs = [col_spec(0), col_spec(1), col_spec(2), tab_spec, tab_spec]
    args = [qkv, qkv, qkv, cos, sin]
    if aliased:
        in_specs.insert(0, pl.BlockSpec(memory_space=pl.ANY))
        args.insert(0, prev_kv)
    return pl.pallas_call(
        functools.partial(_attn_prompt_kernel, s=s, dil=dil, keep=window, units=units, aliased=aliased),
        grid=(b, nslab),
        in_specs=in_specs,
        out_specs=[out_spec, out_spec,
                   pl.BlockSpec((None, None, 2, LANES, window), lambda i, p: (layer, i, 0, p, 0))],
        out_shape=[jax.ShapeDtypeStruct((b, s, A_GW), F32),
                   jax.ShapeDtypeStruct((b, s, A_GW), F32),
                   jax.ShapeDtypeStruct((n_layers, b, 2, A_GW, window), F32)],
        scratch_shapes=[pltpu.VMEM((s, LANES), F32), pltpu.VMEM((s, LANES), F32)],
        input_output_aliases={0: 2} if aliased else {},
        compiler_params=_cparams("parallel", "parallel"),
        name=f"attn_prompt_g{g}",
    )(*args)


def _norm_mm_t_kernel(wt_ref, x_ref, g_ref, o_ref):
    h = _rms(x_ref[...], g_ref[...])
    o_ref[...] = _bdot_nt(wt_ref[...], h)


def norm_mm_t(wt, x, g, tn=768):
    n, d = wt.shape
    m = x.shape[0]
    tn = _row_tile(n, tn)
    return pl.pallas_call(
        _norm_mm_t_kernel,
        grid=(n // tn,),
        in_specs=[pl.BlockSpec((tn, d), lambda i: (i, 0)),
                  pl.BlockSpec((m, d), lambda i: (0, 0)),
                  pl.BlockSpec((1, d), lambda i: (0, 0))],
        out_specs=pl.BlockSpec((tn, m), lambda i: (i, 0)),
        out_shape=jax.ShapeDtypeStruct((n, m), F32),
        compiler_params=_cparams("parallel"),
        name="norm_mm_t",
    )(wt, x, g.reshape(1, d))


def _attn_sample_kernel(*refs, bb, **static):
    if static["aliased"]:
        refs = refs[1:]
    for bl in range(bb):
        _attn_sample_one(bl, pl.program_id(0) * bb + bl, *refs, **static)


def _attn_sample_one(bl, b, q_ref, kt_ref, vt_ref, cos_ref, sin_ref, cost_ref, sint_ref, cache_ref,
                     o_ref, lse_ref, out_ref, *, t_new, length, window, dil, aliased):
    per_blk = LANES // t_new
    new0 = LANES - t_new
    shift = new0 - (b % per_blk) * t_new
    half = HD_A // 2
    ncol = length // LANES
    scale = HD_A ** -0.5
    cos, sin = cos_ref[...], sin_ref[...]
    cos_t, sin_t = cost_ref[...], sint_ref[...]
    q = jnp.concatenate(
        [_rope_slab(q_ref[bl, :, p * LANES:(p + 1) * LANES], cos, sin) for p in range(A_GW // LANES)],
        axis=1) * scale
    q8 = jnp.concatenate([q, jnp.zeros((8 - t_new, A_GW), F32)], axis=0)
    lane = _iota((HD_A, LANES), 1)
    is_new = lane >= new0
    t_b = _iota((8, length), 0)
    l_b = _iota((8, length), 1)
    valid_b = ((l_b & (dil - 1)) == (t_b & (dil - 1))) & (l_b >= t_b + (length - window))
    t_n = _iota((8, LANES), 0)
    u_n = _iota((8, LANES), 1) - new0
    valid_n = (u_n >= 0) & (u_n <= t_n) & (((t_n - u_n) & (dil - 1)) == 0)
    heads = range(HG_A)
    k_new, v_new = [], []
    for h in heads:
        hr = slice(h * HD_A, (h + 1) * HD_A)
        kt = kt_ref[hr, :]
        x1, x2 = kt[:half], kt[half:]
        kt = jnp.concatenate([x1 * cos_t - x2 * sin_t, x2 * cos_t + x1 * sin_t], axis=0)
        k_new.append(jnp.where(is_new, pltpu.roll(kt, shift, 1), 0.0))
        v_new.append(jnp.where(is_new, pltpu.roll(vt_ref[hr, :], shift, 1), 0.0))
    q_h = [q8[:, h * HD_A:(h + 1) * HD_A] for h in heads]
    s_b = [jnp.where(valid_b, _bdot(q_h[h], cache_ref[0, bl, 0, h]), NEG_INF) for h in heads]
    s_n = [jnp.where(valid_n, _bdot(q_h[h], k_new[h]), NEG_INF) for h in heads]
    m = [jnp.maximum(jnp.max(s_b[h], axis=-1, keepdims=True), jnp.max(s_n[h], axis=-1, keepdims=True))
         for h in heads]
    e_b = [jnp.exp(s_b[h] - m[h]) for h in heads]
    e_n = [jnp.exp(s_n[h] - m[h]) for h in heads]
    l = [jnp.sum(e_b[h], axis=-1, keepdims=True) + jnp.sum(e_n[h], axis=-1, keepdims=True) for h in heads]
    o = [(_bdot_nt(e_b[h], cache_ref[0, bl, 1, h]) + _bdot_nt(e_n[h], v_new[h])) / l[h] for h in heads]
    o_ref[bl] = jnp.concatenate(o, axis=1)[:t_new]
    lse_ref[bl] = jnp.concatenate(
        [jnp.broadcast_to(m[h] + jnp.log(l[h]), (8, HD_A)) for h in heads], axis=1)[:t_new]
    for h in heads:
        for c, new in enumerate((k_new[h], v_new[h])):
            rolled = [pltpu.roll(cache_ref[0, bl, c, h, :, i * LANES:(i + 1) * LANES], new0, 1)
                      for i in range(ncol)]
            for i in range(ncol):
                nxt = rolled[i + 1] if i + 1 < ncol else new
                out_ref[0, bl, c, h, :, i * LANES:(i + 1) * LANES] = jnp.where(is_new, nxt, rolled[i])


def attn_sample(g, layer, qkv, kvt_new, tabs, tabs_t, cache_t, prev_out):
    window, dil = A_GROUPS[g]
    bd_, t_new, width = qkv.shape
    n_layers, _, _, _, _, length = cache_t.shape
    assert LANES % t_new == 0 and length % LANES == 0 and t_new <= 8 and dil & (dil - 1) == 0
    per_blk = LANES // t_new
    slabs = width // A_GW
    kv_rows = kvt_new.shape[0] // A_GW
    aliased = prev_out is not None
    seq_bytes = 2 * HG_A * HD_A * length * 4
    bb = max(1, min(SAMPLE_SEQS_PER_STEP, VMEM_LIMIT // 2 // (4 * seq_bytes)))
    assert per_blk % bb == 0 and bd_ % bb == 0
    in_specs = [pl.BlockSpec((bb, t_new, A_GW), lambda i: (i, 0, g * 3)),
                pl.BlockSpec((A_GW, LANES), lambda i: (2 * g, i * bb // per_blk)),
                pl.BlockSpec((A_GW, LANES), lambda i: (2 * g + 1, i * bb // per_blk)),
                pl.BlockSpec((t_new, LANES), lambda i: (0, 0)),
                pl.BlockSpec((t_new, LANES), lambda i: (0, 0)),
                pl.BlockSpec((HD_A // 2, LANES), lambda i: (0, 0)),
                pl.BlockSpec((HD_A // 2, LANES), lambda i: (0, 0)),
                pl.BlockSpec((1, bb, 2, HG_A, HD_A, length), lambda i: (layer, i, 0, 0, 0, 0))]
    args = [qkv, kvt_new, kvt_new, *tabs, *tabs_t, cache_t]
    if aliased:
        in_specs.insert(0, pl.BlockSpec(memory_space=pl.ANY))
        args.insert(0, prev_out)
    assert slabs == 3 * len(A_GROUPS) and kv_rows == 2 * len(A_GROUPS)
    return pl.pallas_call(
        functools.partial(_attn_sample_kernel, bb=bb, t_new=t_new, length=length, window=window,
                          dil=dil, aliased=aliased),
        grid=(bd_ // bb,),
        in_specs=in_specs,
        out_specs=[pl.BlockSpec((bb, t_new, A_GW), lambda i: (i, 0, 0)),
                   pl.BlockSpec((bb, t_new, A_GW), lambda i: (i, 0, 0)),
                   pl.BlockSpec((1, bb, 2, HG_A, HD_A, length), lambda i: (layer, i, 0, 0, 0, 0))],
        out_shape=[jax.ShapeDtypeStruct((bd_, t_new, A_GW), F32),
                   jax.ShapeDtypeStruct((bd_, t_new, A_GW), F32),
                   jax.ShapeDtypeStruct(cache_t.shape, F32)],
        input_output_aliases={0: 2} if aliased else {},
        compiler_params=_cparams("arbitrary"),
        name=f"attn_sample_g{g}",
    )(*args)


def _split3_dot(a01, b):
    a = a01.astype(BF16)
    b1 = b.astype(BF16)
    r1 = b - b1.astype(F32)
    b2 = r1.astype(BF16)
    b3 = (r1 - b2.astype(F32)).astype(BF16)
    dot = functools.partial(jnp.dot, preferred_element_type=F32)
    return dot(a, b1) + dot(a, b2) + dot(a, b3)


def _delta_kernel(q_ref, k_ref, v_ref, z_ref, gt_ref, hq_ref, hk_ref, hv_ref, cw_q, cw_k, cw_v,
                  alog_ref, dtb_ref, nw_ref, s0_ref, o_ref, s_ref, xq, xk, xv,
                  *, t_len, t_valid, chunk, cg, hb, n_heads):
    halo = 8
    for src, hal, dst in ((q_ref, hq_ref, xq), (k_ref, hk_ref, xk), (v_ref, hv_ref, xv)):
        dst[0:halo, :] = hal[0]
        dst[halo:, :] = src[0]
    rows = cg * chunk
    ci = _iota((chunk, chunk), 0)
    si = _iota((chunk, chunk), 1)
    ltri = (ci >= si).astype(F32)
    strict = ci > si
    eye = (ci == si).astype(F32)
    merge_masks = []
    for lg in range(chunk.bit_length() - 1):
        merge_masks.append(((ci >> (lg + 1)) == (si >> (lg + 1)))
                           & (((ci >> lg) & 1) == 1) & (((si >> lg) & 1) == 0))
    lane_r = _iota((rows, LANES), 1)
    lane_c = _iota((chunk, LANES), 1)
    nw = nw_ref[...]
    head_base = pl.program_id(1) * hb

    def conv_act(xs, cw, r0, cols):
        win = xs[pl.ds(r0, rows + halo), cols]
        w = cw[:, cols]
        c = win[halo - (CONV_W - 1):halo - (CONV_W - 1) + rows, :] * w[0:1, :]
        for j in range(1, CONV_W):
            off = halo - (CONV_W - 1) + j
            c = c + win[off:off + rows, :] * w[j:j + 1, :]
        return _silu(c)

    def l2n(x):
        return x * lax.rsqrt(jnp.sum(x * x, axis=-1, keepdims=True) + L2_EPS)

    def lane_col(x, lane_ids, idx):
        return jnp.sum(jnp.where(lane_ids == idx, x, 0.0), axis=-1, keepdims=True)

    def group(i, states):
        r0 = pl.multiple_of(i * rows, rows)
        gate = gt_ref[0, pl.ds(r0, rows), :]
        x = gate + dtb_ref[...]
        softplus = jnp.maximum(x, 0.0) + jnp.log(1.0 + jnp.exp(-jnp.abs(x)))
        g_all = -jnp.exp(alog_ref[...]) * softplus
        beta_all = 1.0 / (1.0 + jnp.exp(-gate))
        if t_valid < t_len:
            valid = (r0 + _iota((rows, 1), 0)) < t_valid
            g_all = jnp.where(valid, g_all, 0.0)
            beta_all = jnp.where(valid, beta_all, 0.0)
        chunks = [slice(c * chunk, (c + 1) * chunk) for c in range(cg)]
        g_cum_all = [_split3_dot(ltri, g_all[rs]) for rs in chunks]
        heads = []
        for hh in range(hb):
            cols = slice(hh * DK_B, (hh + 1) * DK_B)
            k_all = l2n(conv_act(xk, cw_k, r0, cols))
            if t_valid < t_len:
                k_all = jnp.where(valid, k_all, 0.0)
            heads.append((l2n(conv_act(xq, cw_q, r0, cols)) * (DK_B ** -0.5), k_all,
                          conv_act(xv, cw_v, r0, cols),
                          lane_col(beta_all, lane_r, n_heads + head_base + hh)))
        units = [(hh, c) for c in range(cg) for hh in range(hb)]
        q_, k_, v_, kb_, gc_, dec_ = [], [], [], [], [], []
        for hh, c in units:
            q_all, k_all, v_all, beta_col = heads[hh]
            rs = chunks[c]
            g_c = jnp.broadcast_to(lane_col(g_cum_all[c], lane_c, head_base + hh), (chunk, LANES))
            g_r = jnp.transpose(jnp.concatenate(
                [g_c, jnp.zeros((LANES - chunk, LANES), F32)], axis=0))[:chunk, :chunk]
            diff = jnp.where(ci >= si, g_c[:, :chunk] - g_r, 0.0)
            q_.append(q_all[rs]); k_.append(k_all[rs]); v_.append(v_all[rs] * beta_col[rs])
            kb_.append(k_all[rs] * beta_col[rs]); gc_.append(g_c)
            dec_.append(jnp.where(ci >= si, jnp.exp(diff), 0.0))
        n_u = len(units)
        a_ = [jnp.where(strict, _bdot_nt(kb_[u], k_[u]) * dec_[u], 0.0) for u in range(n_u)]
        t_ = [eye - jnp.where(merge_masks[0], a_[u], 0.0) for u in range(n_u)]
        for mask in merge_masks[1:]:
            low = [_bdot(jnp.where(mask, a_[u], 0.0), t_[u]) for u in range(n_u)]
            t_ = [t_[u] - _bdot(t_[u], low[u]) for u in range(n_u)]
        eg_ = [jnp.exp(gc_[u]) for u in range(n_u)]
        uw_ = [_bdot(t_[u], jnp.concatenate([v_[u], kb_[u] * eg_[u]], axis=1)) for u in range(n_u)]
        attn_ = [_bdot_nt(q_[u], k_[u]) * dec_[u] for u in range(n_u)]
        states = list(states)
        for c in range(cg):
            us = [u for u in range(n_u) if units[u][1] == c]
            glast = {u: gc_[u][chunk - 1:chunk, :] for u in us}
            ws = {u: _bdot(uw_[u][:, DK_B:], states[units[u][0]]) for u in us}
            qs = {u: _bdot(q_[u] * eg_[u], states[units[u][0]]) for u in us}
            vn = {u: uw_[u][:, :DK_B] - ws[u] for u in us}
            for u in us:
                hh = units[u][0]
                kd = k_[u] * jnp.exp(glast[u] - gc_[u])
                o = qs[u] + _bdot(attn_[u], vn[u])
                states[hh] = states[hh] * jnp.exp(glast[u]) + _bdot_tn(kd, vn[u])
                o = o * lax.rsqrt(jnp.mean(o * o, axis=-1, keepdims=True) + RMS_EPS) * nw
                cols = slice(hh * DK_B, (hh + 1) * DK_B)
                z = z_ref[0, pl.ds(r0 + c * chunk, chunk), cols]
                o_ref[0, pl.ds(r0 + c * chunk, chunk), cols] = o * _silu(z)
        return tuple(states)

    init = tuple(s0_ref[0, hh] for hh in range(hb))
    final = lax.fori_loop(0, t_len // rows, group, init)
    for hh in range(hb):
        s_ref[0, hh] = final[hh]


def delta_core(proj, conv_halo, s0, conv_w, a_log, dt_bias, norm_w, *, t_valid, chunk, cg, hb):
    b, t_len, width = proj.shape
    n_heads = s0.shape[1]
    hw = n_heads * DK_B
    assert width == 4 * hw + LANES and t_len % (cg * chunk) == 0 and n_heads % hb == 0
    nhb = n_heads // hb
    bw = hb * DK_B

    def col_spec(rows, c):
        return pl.BlockSpec((1, rows, bw), lambda i, h: (i, 0, c * nhb + h))

    def row_vec(x):
        return jnp.pad(x.astype(F32), (0, LANES - x.shape[0])).reshape(1, LANES)

    vec_spec = pl.BlockSpec((1, LANES), lambda i, h: (0, 0))
    cw_specs = [pl.BlockSpec((CONV_W, bw), lambda i, h, c=c: (0, c * nhb + h)) for c in range(3)]
    return pl.pallas_call(
        functools.partial(_delta_kernel, t_len=t_len, t_valid=t_valid, chunk=chunk, cg=cg,
                          hb=hb, n_heads=n_heads),
        grid=(b, nhb),
        in_specs=[col_spec(t_len, 0), col_spec(t_len, 1), col_spec(t_len, 2), col_spec(t_len, 3),
                  pl.BlockSpec((1, t_len, LANES), lambda i, h: (i, 0, 4 * hw // LANES)),
                  col_spec(8, 0), col_spec(8, 1), col_spec(8, 2)] + cw_specs
                 + [vec_spec, vec_spec, vec_spec,
                    pl.BlockSpec((1, hb, DK_B, DK_B), lambda i, h: (i, h, 0, 0))],
        out_specs=[pl.BlockSpec((1, t_len, bw), lambda i, h: (i, 0, h)),
                   pl.BlockSpec((1, hb, DK_B, DK_B), lambda i, h: (i, h, 0, 0))],
        out_shape=[jax.ShapeDtypeStruct((b, t_len, hw), F32),
                   jax.ShapeDtypeStruct(s0.shape, F32)],
        scratch_shapes=[pltpu.VMEM((t_len + 8, bw), F32)] * 3,
        compiler_params=_cparams("parallel", "parallel"),
        name="delta_core",
    )(proj, proj, proj, proj, proj, conv_halo, conv_halo, conv_halo, conv_w, conv_w, conv_w,
      row_vec(a_log), row_vec(dt_bias), norm_w.astype(F32).reshape(1, DK_B), s0)


def _pool_prompt_kernel(x_ref, xh_ref, g_ref, w_ref, sc_ref, o_ref, tail_ref, hs, *, tm):
    i = pl.program_id(1)
    g = g_ref[...]
    x = x_ref[0]
    h = _rms(x, g)
    hs[0:16, :] = jnp.where(i > 0, _rms(xh_ref[0], g), 0.0)
    hs[16:, :] = h
    tail_ref[0] = h[tm - 16:, :]
    pos = (i * tm + _iota((tm, 1), 0) + 1).astype(F32)
    cg = x.shape[1] // len(POOL_WINDOWS)
    for gi, w in enumerate(POOL_WINDOWS):
        cols = slice(gi * cg, (gi + 1) * cg)
        tot = h[:, cols]
        for j in range(1, w):
            tot = tot + hs[16 - j:16 - j + tm, cols]
        d = tot / jnp.minimum(float(w), pos) - h[:, cols]
        o_ref[0, :, cols] = x[:, cols] + _bdot(d, w_ref[gi]) * sc_ref[:, cols]


def pool_prompt(x, g, w_grp, scale, tm=512):
    b, s, d = x.shape
    tm = _row_tile(s, tm)
    per = tm // 16
    ng, cgw, _ = w_grp.shape
    return pl.pallas_call(
        functools.partial(_pool_prompt_kernel, tm=tm),
        grid=(b, s // tm),
        in_specs=[pl.BlockSpec((1, tm, d), lambda i, j: (i, j, 0)),
                  pl.BlockSpec((1, 16, d), lambda i, j: (i, jnp.maximum(j * per - 1, 0), 0)),
                  pl.BlockSpec((1, d), lambda i, j: (0, 0)),
                  pl.BlockSpec((ng, cgw, cgw), lambda i, j: (0, 0, 0)),
                  pl.BlockSpec((1, d), lambda i, j: (0, 0))],
        out_specs=[pl.BlockSpec((1, tm, d), lambda i, j: (i, j, 0)),
                   pl.BlockSpec((1, 16, d), lambda i, j: (i, 0, 0))],
        out_shape=[jax.ShapeDtypeStruct((b, s, d), F32),
                   jax.ShapeDtypeStruct((b, 16, d), F32)],
        scratch_shapes=[pltpu.VMEM((tm + 16, d), F32)],
        compiler_params=_cparams("parallel", "arbitrary"),
        name="pool_prompt",
    )(x, x, g.reshape(1, d), w_grp, scale.reshape(1, d))


def _pool_sample_kernel(x_ref, pre_ref, g_ref, w_ref, sc_ref, o_ref, h_ref, *, t_new, n_pre):
    g = g_ref[...]
    xs = [x_ref[t] for t in range(t_new)]
    rows = [pre_ref[r] for r in range(n_pre)] + [_rms(x, g) for x in xs]
    cg = xs[0].shape[1] // len(POOL_WINDOWS)
    for t in range(t_new):
        h_ref[t] = rows[n_pre + t]
        for gi, w in enumerate(POOL_WINDOWS):
            cols = slice(gi * cg, (gi + 1) * cg)
            lo = max(n_pre + t - w + 1, 0)
            tot = rows[lo][:, cols]
            for r in range(lo + 1, n_pre + t + 1):
                tot = tot + rows[r][:, cols]
            d = tot / float(min(w, t + 1 + n_pre)) - rows[n_pre + t][:, cols]
            o_ref[t, :, cols] = xs[t][:, cols] + _bdot(d, w_ref[gi]) * sc_ref[:, cols]


def pool_sample(x_t, pre_t, g, w_grp, scale):
    t_new, bd_, d = x_t.shape
    n_pre = pre_t.shape[0]
    ng, cgw, _ = w_grp.shape
    full = lambda shape: pl.BlockSpec(shape, lambda i: (0,) * len(shape))
    return pl.pallas_call(
        functools.partial(_pool_sample_kernel, t_new=t_new, n_pre=n_pre),
        grid=(1,),
        in_specs=[full((t_new, bd_, d)), full((n_pre, bd_, d)), full((1, d)),
                  full((ng, cgw, cgw)), full((1, d))],
        out_specs=[full((t_new, bd_, d)), full((t_new, bd_, d))],
        out_shape=[jax.ShapeDtypeStruct((t_new, bd_, d), F32),
                   jax.ShapeDtypeStruct((t_new, bd_, d), F32)],
        compiler_params=_cparams("arbitrary"),
        name="pool_sample",
    )(x_t, pre_t, g.reshape(1, d), w_grp, scale.reshape(1, d))


def _rope_tables_t(pos, n_cols):
    half = HD_A // 2
    inv = ROPE_THETA ** (-jnp.arange(half, dtype=F32) / half)
    ang = inv[:, None] * jnp.tile(pos.astype(F32), n_cols // pos.shape[0])[None, :]
    return jnp.cos(ang), jnp.sin(ang)


def _mixer_a(xp, xs, g1, w_in, wt_kv, layer, caches_t, prev_kv_p, prev_outs,
             tabs_p, tabs_s, tabs_st):
    bp, sp, d = xp.shape
    bs, ts, _ = xs.shape
    n_layers = caches_t[0].shape[0]
    xp2, xs2 = xp.reshape(-1, d), xs.reshape(-1, d)
    qkv_p = norm_mm(xp2, g1, w_in).reshape(bp, sp, -1)
    qkv_s = norm_mm(xs2, g1, w_in).reshape(bs, ts, -1)
    kvt_new = norm_mm_t(wt_kv, xs2, g1)
    os_, lses, kv_p = [], [], []
    os_s, lses_s, kv_s = [], [], []
    for g, (window, dil) in enumerate(A_GROUPS):
        assert sp >= window and caches_t[g].shape[-1] >= window
        o, lse, kv = attn_prompt(qkv_p, g, layer, n_layers,
                                 None if prev_kv_p is None else prev_kv_p[g], *tabs_p)
        os_.append(o.reshape(-1, A_GW))
        lses.append(lse.reshape(-1, A_GW))
        kv_p.append(kv)
        o, lse, out = attn_sample(g, layer, qkv_s, kvt_new, tabs_s, tabs_st, caches_t[g],
                                  None if prev_outs is None else prev_outs[g])
        os_s.append(o.reshape(-1, A_GW))
        lses_s.append(lse.reshape(-1, A_GW))
        kv_s.append(out)
    return (os_, lses), (os_s, lses_s), kv_p, kv_s


def _mixer_b(xp, xs, g1, w_in, conv_w, a_log, dt_bias, norm_w, conv_s0, state_s0):
    bp, sp, d = xp.shape
    bs, ts, _ = xs.shape
    n_heads = state_s0.shape[1]
    qkv_w = 3 * n_heads * DK_B
    xp2, xs2 = xp.reshape(-1, d), xs.reshape(-1, d)
    proj_p = norm_mm(xp2, g1, w_in).reshape(bp, sp, -1)
    proj_s = norm_mm(xs2, g1, w_in).reshape(bs, ts, -1)
    consts = (conv_w, a_log, dt_bias, norm_w)
    o_p, st_p = delta_core(proj_p, jnp.zeros((bp, 8, qkv_w), F32),
                           jnp.zeros((bp,) + state_s0.shape[1:], F32), *consts,
                           t_valid=sp, chunk=CHUNK_B, cg=8, hb=2)
    t_pad = 8
    assert CONV_W - 1 <= ts <= t_pad
    o_s, st_s = delta_core(jnp.pad(proj_s, ((0, 0), (0, t_pad - ts), (0, 0))),
                           jnp.pad(conv_s0, ((0, 0), (8 - (CONV_W - 1), 0), (0, 0))),
                           state_s0, *consts, t_valid=ts, chunk=t_pad, cg=1, hb=n_heads)
    conv_p = proj_p[:, sp - (CONV_W - 1):, :qkv_w]
    conv_s = proj_s[:, ts - (CONV_W - 1):, :qkv_w]
    return (([o_p.reshape(bp * sp, -1)], []), ([o_s[:, :ts].reshape(bs * ts, -1)], []),
            (st_p, conv_p), (st_s, conv_s))


def _mixer_c(xp, xs, g1, w_grp, scale, pool_s0):
    ts = xs.shape[1]
    yp, tail = pool_prompt(xp, g1, w_grp, scale)
    pool_p = tail[:, tail.shape[1] - POOL_BUF:]
    y_t, h_t = pool_sample(xs.transpose(1, 0, 2), pool_s0.transpose(1, 0, 2), g1, w_grp, scale)
    pool_s = jnp.concatenate([pool_s0, h_t.transpose(1, 0, 2)], axis=1)[:, ts:]
    return yp, y_t.transpose(1, 0, 2), pool_p, pool_s


def kernel(x_prompt, x_sample, cache_a_kv1, cache_a_kv2, cache_a_kv3, state_b_s, state_b_conv, state_c_pool, norm1, norm2, norm_f, a_w_in, a_w_out, b_w_in, b_conv, b_a_log, b_dt_bias, b_norm, b_w_out, c_w, c_scale, mlp_w1, mlp_w2):
    xp, xs = x_prompt, x_sample
    depth = norm1.shape[0]
    d = xp.shape[-1]
    past_len = cache_a_kv3.shape[2]
    tabs_p = _rope_tables(jnp.arange(xp.shape[1]))
    pos_s = past_len + jnp.arange(xs.shape[1])
    tabs_s = _rope_tables(pos_s)
    tabs_st = _rope_tables_t(pos_s, LANES)
    caches_t = [jnp.transpose(c, (0, 1, 3, 4, 5, 2)) for c in (cache_a_kv1, cache_a_kv2, cache_a_kv3)]
    kv_cols = np.concatenate([np.arange(g * 3 * A_GW + A_GW, (g + 1) * 3 * A_GW)
                              for g in range(len(A_GROUPS))])
    b_pad = (-b_w_in.shape[2]) % LANES
    b_p, b_s, c_p, c_s = [], [], [], []
    kv_p = kv_s = None
    for i in range(depth):
        kind, j = i % 3, i // 3
        mlp = (norm2[i], mlp_w1[i].astype(BF16), mlp_w2[i].astype(BF16), norm_f, i == depth - 1)
        xp2, xs2 = xp.reshape(-1, d), xs.reshape(-1, d)
        if kind == 2:
            xp, xs, pool_p, pool_s = _mixer_c(xp, xs, norm1[i], c_w[j].astype(BF16), c_scale[j],
                                              state_c_pool[j])
            c_p.append(pool_p)
            c_s.append(pool_s)
            xp = mlp_res(xp.reshape(-1, d), *mlp).reshape(xp.shape)
            xs = mlp_res(xs.reshape(-1, d), *mlp).reshape(xs.shape)
            continue
        if kind == 0:
            w_in = a_w_in[j].astype(BF16)
            w_out = a_w_out[j].astype(BF16)
            mix_p, mix_s, kv_p, kv_s = _mixer_a(xp, xs, norm1[i], w_in, w_in[:, kv_cols].T, j,
                                                caches_t, kv_p, kv_s, tabs_p, tabs_s, tabs_st)
        else:
            w_in = jnp.pad(b_w_in[j], ((0, 0), (0, b_pad))).astype(BF16)
            w_out = b_w_out[j].astype(BF16)
            mix_p, mix_s, st_p, st_s = _mixer_b(xp, xs, norm1[i], w_in, b_conv[j], b_a_log[j],
                                                b_dt_bias[j], b_norm[j], state_b_conv[j], state_b_s[j])
            b_p.append(st_p)
            b_s.append(st_s)
        xp = proj_mlp_res(*mix_p, w_out, xp2, *mlp).reshape(xp.shape)
        xs = proj_mlp_res(*mix_s, w_out, xs2, *mlp).reshape(xs.shape)
    stack = lambda items, k: jnp.stack([e[k] for e in items])
    bp = xp.shape[0]
    kv_p = [jnp.transpose(c.reshape(c.shape[0], bp, 2, HG_A, HD_A, c.shape[-1]), (0, 1, 5, 2, 3, 4))
            for c in kv_p]
    return (xp, xs, *kv_p,
            stack(b_p, 0), stack(b_p, 1), jnp.stack(c_p),
            *[jnp.transpose(c, (0, 1, 5, 2, 3, 4)) for c in kv_s],
            stack(b_s, 0), stack(b_s, 1), jnp.stack(c_s))
```
